```python
import math
import jax, jax.numpy as jnp
from jax import lax
import numpy as np

D_MODEL = 2048
BATCH = 4
SEQ = 2048
DEPTH = 4
DEC_BATCH = 128
DEC_SEQ = 4
PAST_LEN = 16384
PAGE_SIZE = 128

D_SSM = D_MODEL // 2
SSM_GROUP = 16
N_SSM_GROUPS = D_SSM // SSM_GROUP
SSM_STATE = 64
DT_MIN = 1e-3
DT_MAX = 1e-1
D_GMLP = D_MODEL - D_SSM
GMLP_HEAD = 128
N_GMLP_HEADS = D_GMLP // GMLP_HEAD
CHUNK = 128
D_IN = D_SSM + 2 * D_GMLP
D_MIX = D_SSM + D_GMLP
N_EXPERTS = 32
TOP_K = 4
D_FF = D_MODEL
SWIGLU_ALPHA = 1.702
SWIGLU_LIMIT = 7.0
MOE_BLOCK = 128
N_MOD = 6
EPS = 1e-5

kernel_name = 'hymba_s5_gmlp_moe_adaln_step'


def rmsnorm(x, g):
    xf = x.astype(jnp.float32)
    y = xf * lax.rsqrt(jnp.mean(xf * xf, axis=-1, keepdims=True) + EPS)
    return (y * g.astype(jnp.float32)).astype(x.dtype)


def layernorm(x, g, b):
    xf = x.astype(jnp.float32)
    mu = jnp.mean(xf, axis=-1, keepdims=True)
    var = jnp.mean(jnp.square(xf - mu), axis=-1, keepdims=True)
    y = (xf - mu) * lax.rsqrt(var + EPS)
    return (y * g.astype(jnp.float32) + b.astype(jnp.float32)).astype(x.dtype)


def _complex_affine_combine(earlier, later):
    a_re1, a_im1, b_re1, b_im1 = earlier
    a_re2, a_im2, b_re2, b_im2 = later
    return (a_re2 * a_re1 - a_im2 * a_im1,
            a_re2 * a_im1 + a_im2 * a_re1,
            a_re2 * b_re1 - a_im2 * b_im1 + b_re2,
            a_re2 * b_im1 + a_im2 * b_re1 + b_im2)


def s5_mixer(u, h0_re, h0_im, lam_re, lam_im, log_dt, b_re, b_im, c_re, c_im, d_skip, w_glu, b_glu):
    bsz, seq = u.shape[0], u.shape[1]
    uf = u.astype(jnp.float32).reshape(bsz, seq, N_SSM_GROUPS, SSM_GROUP)
    lr = lam_re.astype(jnp.float32)
    li = lam_im.astype(jnp.float32)
    dt = jnp.exp(log_dt.astype(jnp.float32))[:, None]
    mag = jnp.exp(lr * dt)
    ab_re = mag * jnp.cos(li * dt)
    ab_im = mag * jnp.sin(li * dt)
    den = lr * lr + li * li
    q_re = ((ab_re - 1.0) * lr + ab_im * li) / den
    q_im = (ab_im * lr - (ab_re - 1.0) * li) / den
    br = b_re.astype(jnp.float32)
    bi = b_im.astype(jnp.float32)
    bb_re = q_re[..., None] * br - q_im[..., None] * bi
    bb_im = q_re[..., None] * bi + q_im[..., None] * br
    bu_re = jnp.einsum('gpc,btgc->btgp', bb_re, uf)
    bu_im = jnp.einsum('gpc,btgc->btgp', bb_im, uf)
    a_re = jnp.broadcast_to(ab_re, bu_re.shape)
    a_im = jnp.broadcast_to(ab_im, bu_im.shape)
    acc_re, acc_im, s_re, s_im = lax.associative_scan(
        _complex_affine_combine, (a_re, a_im, bu_re, bu_im), axis=1)
    g_re = h0_re.astype(jnp.float32)[:, None]
    g_im = h0_im.astype(jnp.float32)[:, None]
    h_re = s_re + acc_re * g_re - acc_im * g_im
    h_im = s_im + acc_re * g_im + acc_im * g_re
    y = (jnp.einsum('gcp,btgp->btgc', c_re.astype(jnp.float32), h_re)
         - jnp.einsum('gcp,btgp->btgc', c_im.astype(jnp.float32), h_im)
         + d_skip.astype(jnp.float32) * uf)
    y = jax.nn.gelu(y.reshape(bsz, seq, D_SSM))
    y = y * jax.nn.sigmoid(y @ w_glu.astype(jnp.float32) + b_glu.astype(jnp.float32))
    return (y.astype(u.dtype), h_re[:, -1].astype(h0_re.dtype), h_im[:, -1].astype(h0_im.dtype))


def chunk_gmlp(u, v, ln_g, ln_b, w_s, b_s, block_len, offset):
    bsz, seq = v.shape[0], v.shape[1]
    u = jax.nn.gelu(u)
    v = layernorm(jax.nn.gelu(v), ln_g, ln_b)
    ws = jnp.tril(w_s)[:, offset:offset + block_len, offset:offset + block_len]
    bs = b_s[:, offset:offset + block_len]
    vc = v.reshape(bsz, seq // block_len, block_len, N_GMLP_HEADS, GMLP_HEAD)
    mix = jnp.einsum('hts,bcshd->bcthd', ws, vc) + bs.T[None, None, :, :, None]
    return u * mix.reshape(bsz, seq, D_GMLP), v


def moe(h, w_router, b_router, w1, b1, w2, b2):
    bsz, seq, dim = h.shape
    x = h.reshape(-1, dim)
    n_tok = x.shape[0]
    logits = (x @ w_router + b_router).astype(jnp.float32)
    top_logits, top_idx = lax.top_k(logits, TOP_K)
    gates = jax.nn.softmax(top_logits, axis=-1)
    n_assign = n_tok * TOP_K
    flat_e = top_idx.reshape(-1)
    flat_tok = jnp.arange(n_assign, dtype=jnp.int32) // TOP_K
    flat_gate = gates.reshape(-1)
    order = jnp.argsort(flat_e)
    sorted_e = flat_e[order]
    counts = jnp.zeros((N_EXPERTS,), jnp.int32).at[flat_e].add(1)
    padded = (counts + MOE_BLOCK - 1) // MOE_BLOCK * MOE_BLOCK
    pad_end = jnp.cumsum(padded)
    pad_start = pad_end - padded
    start = jnp.cumsum(counts) - counts
    dest = pad_start[sorted_e] + jnp.arange(n_assign, dtype=jnp.int32) - start[sorted_e]
    n_blocks = -(-(n_assign + N_EXPERTS * (MOE_BLOCK - 1)) // MOE_BLOCK)
    n_rows = n_blocks * MOE_BLOCK
    row_tok = jnp.full((n_rows,), n_tok, jnp.int32).at[dest].set(flat_tok[order])
    row_gate = jnp.zeros((n_rows,), jnp.float32).at[dest].set(flat_gate[order])
    block_e = jnp.minimum(
        jnp.searchsorted(pad_end, jnp.arange(n_blocks, dtype=jnp.int32) * MOE_BLOCK, side='right'),
        N_EXPERTS - 1)
    x_pad = jnp.concatenate([x, jnp.zeros((1, dim), x.dtype)], axis=0)

    def expert_block(args):
        tok, e = args
        a = x_pad[tok] @ w1[e] + b1[e]
        glu, lin = jnp.split(a, 2, axis=-1)
        glu = jnp.minimum(glu, SWIGLU_LIMIT)
        lin = jnp.clip(lin, -SWIGLU_LIMIT, SWIGLU_LIMIT)
        return (glu * jax.nn.sigmoid(SWIGLU_ALPHA * glu) * (lin + 1.0)) @ w2[e] + b2[e]

    y_rows = lax.map(expert_block, (row_tok.reshape(n_blocks, MOE_BLOCK), block_e))
    y_rows = y_rows.reshape(n_rows, dim).astype(jnp.float32) * row_gate[:, None]
    y = jnp.zeros((n_tok + 1, dim), jnp.float32).at[row_tok].add(y_rows)[:n_tok]
    return y.reshape(bsz, seq, dim).astype(h.dtype)


def trunk(x, c, h0_re, h0_im, block_len, offset, p):
    bsz = x.shape[0]
    new_re, new_im, new_v = [], [], []
    for l in range(DEPTH):
        mod = (jax.nn.silu(c) @ p['w_ada'][l] + p['b_ada'][l]).reshape(bsz, N_MOD, D_MODEL)
        sh_m, sc_m, g_m, sh_f, sc_f, g_f = [mod[:, i, None, :] for i in range(N_MOD)]
        h = rmsnorm(x, p['g_mix'][l]) * (1.0 + sc_m) + sh_m
        proj = h @ p['w_in'][l]
        u_a = proj[..., :D_SSM]
        u_b = proj[..., D_SSM:D_SSM + D_GMLP]
        v_b = proj[..., D_SSM + D_GMLP:]
        y_a, hr, hi = s5_mixer(u_a, h0_re[l], h0_im[l], p['lam_re'][l], p['lam_im'][l], p['log_dt'][l],
                               p['b_re'][l], p['b_im'][l], p['c_re'][l], p['c_im'][l], p['d_skip'][l],
                               p['w_glu'][l], p['b_glu'][l])
        y_b, v_rows = chunk_gmlp(u_b, v_b, p['ln_v_g'][l], p['ln_v_b'][l], p['w_s'][l], p['b_s'][l],
                                 block_len, offset)
        mixed = jnp.concatenate([rmsnorm(y_a, p['g_out_a'][l]), rmsnorm(y_b, p['g_out_b'][l])], axis=-1)
        x = x + g_m * (mixed @ p['w_out'][l])
        h = rmsnorm(x, p['g_ffn'][l]) * (1.0 + sc_f) + sh_f
        x = x + g_f * moe(h, p['w_router'][l], p['b_router'][l], p['w1'][l], p['b1'][l], p['w2'][l], p['b2'][l])
        new_re.append(hr)
        new_im.append(hi)
        new_v.append(v_rows)
    return rmsnorm(x, p['g_final']), jnp.stack(new_re), jnp.stack(new_im), new_v


def setup_inputs(seed: int = 0) -> dict:
    key = jax.random.key(seed)
    ks = jax.random.split(key, 40)
    nrm = jax.random.normal
    f32 = jnp.float32
    d = {}
    d['x_prompt'] = nrm(ks[0], (BATCH, SEQ, D_MODEL), f32)
    d['x_sample'] = nrm(ks[1], (DEC_BATCH, DEC_SEQ, D_MODEL), f32)
    d['c_prompt'] = nrm(ks[2], (BATCH, D_MODEL), f32)
    d['c_sample'] = nrm(ks[3], (DEC_BATCH, D_MODEL), f32)
    d['state_ssm_re'] = 0.3 * nrm(ks[4], (DEPTH, DEC_BATCH, N_SSM_GROUPS, SSM_STATE), f32)
    d['state_ssm_im'] = 0.3 * nrm(ks[5], (DEPTH, DEC_BATCH, N_SSM_GROUPS, SSM_STATE), f32)
    d['g_mix'] = 1.0 + 0.02 * nrm(ks[6], (DEPTH, D_MODEL), f32)
    d['w_ada'] = 0.5 * D_MODEL ** -0.5 * nrm(ks[7], (DEPTH, D_MODEL, N_MOD * D_MODEL), f32)
    d['b_ada'] = 0.02 * nrm(ks[8], (DEPTH, N_MOD * D_MODEL), f32)
    d['w_in'] = D_MODEL ** -0.5 * nrm(ks[9], (DEPTH, D_MODEL, D_IN), f32)
    d['lam_re'] = -0.5 + 0.01 * nrm(ks[10], (DEPTH, N_SSM_GROUPS, SSM_STATE), f32)
    d['lam_im'] = (math.pi * jnp.arange(SSM_STATE, dtype=f32)
                   + 0.01 * nrm(ks[11], (DEPTH, N_SSM_GROUPS, SSM_STATE), f32))
    d['log_dt'] = jax.random.uniform(ks[12], (DEPTH, N_SSM_GROUPS), f32,
                                     math.log(DT_MIN), math.log(DT_MAX))
    d['b_re'] = (2 * SSM_GROUP) ** -0.5 * nrm(ks[13], (DEPTH, N_SSM_GROUPS, SSM_STATE, SSM_GROUP), f32)
    d['b_im'] = (2 * SSM_GROUP) ** -0.5 * nrm(ks[14], (DEPTH, N_SSM_GROUPS, SSM_STATE, SSM_GROUP), f32)
    d['c_re'] = (2 * SSM_STATE) ** -0.5 * nrm(ks[15], (DEPTH, N_SSM_GROUPS, SSM_GROUP, SSM_STATE), f32)
    d['c_im'] = (2 * SSM_STATE) ** -0.5 * nrm(ks[16], (DEPTH, N_SSM_GROUPS, SSM_GROUP, SSM_STATE), f32)
    d['d_skip'] = nrm(ks[17], (DEPTH, N_SSM_GROUPS, SSM_GROUP), f32)
    d['w_glu'] = D_SSM ** -0.5 * nrm(ks[18], (DEPTH, D_SSM, D_SSM), f32)
    d['b_glu'] = 0.02 * nrm(ks[19], (DEPTH, D_SSM), f32)
    d['ln_v_g'] = 1.0 + 0.02 * nrm(ks[20], (DEPTH, D_GMLP), f32)
    d['ln_v_b'] = 0.02 * nrm(ks[21], (DEPTH, D_GMLP), f32)
    d['w_s'] = CHUNK ** -0.5 * nrm(ks[22], (DEPTH, N_GMLP_HEADS, CHUNK, CHUNK), f32)
    d['b_s'] = 1.0 + 0.02 * nrm(ks[23], (DEPTH, N_GMLP_HEADS, CHUNK), f32)
    d['g_out_a'] = 1.0 + 0.02 * nrm(ks[24], (DEPTH, D_SSM), f32)
    d['g_out_b'] = 1.0 + 0.02 * nrm(ks[25], (DEPTH, D_GMLP), f32)
    d['w_out'] = D_MIX ** -0.5 * nrm(ks[26], (DEPTH, D_MIX, D_MODEL), f32)
    d['g_ffn'] = 1.0 + 0.02 * nrm(ks[27], (DEPTH, D_MODEL), f32)
    d['w_router'] = D_MODEL ** -0.5 * nrm(ks[28], (DEPTH, D_MODEL, N_EXPERTS), f32)
    d['b_router'] = 0.01 * nrm(ks[29], (DEPTH, N_EXPERTS), f32)
    d['w1'] = D_MODEL ** -0.5 * nrm(ks[30], (DEPTH, N_EXPERTS, D_MODEL, 2 * D_FF), f32)
    d['b1'] = 0.02 * nrm(ks[31], (DEPTH, N_EXPERTS, 2 * D_FF), f32)
    d['w2'] = D_FF ** -0.5 * nrm(ks[32], (DEPTH, N_EXPERTS, D_FF, D_MODEL), f32)
    d['b2'] = 0.02 * nrm(ks[33], (DEPTH, N_EXPERTS, D_MODEL), f32)
    d['g_final'] = 1.0 + 0.02 * nrm(ks[34], (D_MODEL,), f32)
    return d


def reference(x_prompt, x_sample, c_prompt, c_sample, state_ssm_re, state_ssm_im,
              g_mix, w_ada, b_ada, w_in, lam_re, lam_im, log_dt, b_re, b_im, c_re, c_im, d_skip,
              w_glu, b_glu, ln_v_g, ln_v_b, w_s, b_s, g_out_a, g_out_b, w_out, g_ffn,
              w_router, b_router, w1, b1, w2, b2, g_final):
    p = dict(g_mix=g_mix, w_ada=w_ada, b_ada=b_ada, w_in=w_in, lam_re=lam_re, lam_im=lam_im,
             log_dt=log_dt, b_re=b_re, b_im=b_im, c_re=c_re, c_im=c_im, d_skip=d_skip,
             w_glu=w_glu, b_glu=b_glu, ln_v_g=ln_v_g, ln_v_b=ln_v_b, w_s=w_s, b_s=b_s,
             g_out_a=g_out_a, g_out_b=g_out_b, w_out=w_out, g_ffn=g_ffn, w_router=w_router,
             b_router=b_router, w1=w1, b1=b1, w2=w2, b2=b2, g_final=g_final)
    zeros_state = jnp.zeros((DEPTH, x_prompt.shape[0], N_SSM_GROUPS, SSM_STATE), state_ssm_re.dtype)
    y_prompt, new_ssm_re_prompt, new_ssm_im_prompt, _ = trunk(
        x_prompt, c_prompt, zeros_state, zeros_state, CHUNK, 0, p)
    y_sample, new_ssm_re_sample, new_ssm_im_sample, v_sample = trunk(
        x_sample, c_sample, state_ssm_re, state_ssm_im, x_sample.shape[1], PAST_LEN % CHUNK, p)
    new_v_sample = jnp.stack(v_sample)
    return (y_prompt, y_sample, new_ssm_re_prompt, new_ssm_im_prompt,
            new_ssm_re_sample, new_ssm_im_sample, new_v_sample)
```

```python
import functools

import jax
import jax.numpy as jnp
from jax import lax
from jax.experimental import pallas as pl
from jax.experimental.pallas import tpu as pltpu

F32 = jnp.float32
BF16 = jnp.bfloat16
I32 = jnp.int32

D_MODEL = 2048
BATCH = 4
SEQ = 2048
DEPTH = 4
DEC_BATCH = 128
DEC_SEQ = 4
D_SSM = 1024
SSM_GROUP = 16
N_SSM_GROUPS = 64
SSM_STATE = 64
D_GMLP = 1024
GMLP_HEAD = 128
N_GMLP_HEADS = 8
CHUNK = 128
D_IN = D_SSM + 2 * D_GMLP
N_EXPERTS = 32
TOP_K = 4
D_FF = D_MODEL
SWIGLU_ALPHA = 1.702
SWIGLU_LIMIT = 7.0
N_MOD = 6
EPS = 1e-5

LANES = 128
SUBLANES = 8
N_P = BATCH * SEQ
N_S = DEC_BATCH * DEC_SEQ
N_TOK = N_P + N_S
N_ASSIGN = N_TOK * TOP_K
C_ROWS = 8 + DEC_BATCH
MOD_ROWS = 128

TM = 256
TM_MIX = CHUNK * BATCH
TM_COMB = 128
ROW_TILES = D_MODEL // LANES

GROUP_BLOCK = 16
N_GB = N_SSM_GROUPS // GROUP_BLOCK
GB_CH = GROUP_BLOCK * SSM_GROUP
GB_ST = GROUP_BLOCK * SSM_STATE
SCAN_LANES = 512

ETM = 1024
ESB = 256
N_ESB = ETM // ESB
N_ETILES = N_ASSIGN // ETM + N_EXPERTS
E_ROWS = N_ETILES * ETM
TF = 512
NJ1 = D_FF // TF
TN2 = 512
NJ2 = D_MODEL // TN2
GATHER_BLK = 512

VMEM_LIMIT = 56 * 1024 * 1024


def _cparams(*sem):
    return pltpu.CompilerParams(dimension_semantics=sem, vmem_limit_bytes=VMEM_LIMIT)


def _const_spec(shape):
    zeros = (0,) * len(shape)
    return pl.BlockSpec(shape, lambda *_: zeros, pipeline_mode=pl.Buffered(1))


def _rms(x):
    return x * lax.rsqrt(jnp.mean(x * x, axis=-1, keepdims=True) + EPS)


def _modulate(xn, sc_ref, sh_ref):
    rows, dim = xn.shape
    x3 = xn.reshape(rows // MOD_ROWS, MOD_ROWS, dim)
    return (x3 * (1.0 + sc_ref[0][None]) + sh_ref[0][None]).reshape(rows, dim)


def _gate(v, g_ref):
    rows, dim = v.shape
    return (v.reshape(rows // MOD_ROWS, MOD_ROWS, dim) * g_ref[0][None]).reshape(rows, dim)


def _ada_kernel(c_ref, w_ref, b_ref, o_ref):
    s = jax.nn.silu(c_ref[...]).astype(BF16)
    o_ref[0] = jnp.dot(s, w_ref[0].astype(BF16), preferred_element_type=F32) + b_ref[0]


def _ada_mod(c_all, w_ada, b_ada):
    tn = 1024
    nj = N_MOD * D_MODEL // tn
    return pl.pallas_call(
        _ada_kernel,
        grid=(DEPTH, nj),
        in_specs=[
            pl.BlockSpec((C_ROWS, D_MODEL), lambda l, j: (0, 0)),
            pl.BlockSpec((1, D_MODEL, tn), lambda l, j: (l, 0, j)),
            pl.BlockSpec((1, 1, tn), lambda l, j: (l, 0, j)),
        ],
        out_specs=pl.BlockSpec((1, C_ROWS, tn), lambda l, j: (l, 0, j)),
        out_shape=jax.ShapeDtypeStruct((DEPTH, C_ROWS, N_MOD * D_MODEL), F32),
        compiler_params=_cparams("arbitrary", "arbitrary"),
        name="ada_mod",
    )(c_all, w_ada, b_ada.reshape(DEPTH, 1, N_MOD * D_MODEL))


def _disc_kernel(lr_ref, li_ref, ldt_ref, br_ref, bi_ref, abr_ref, abi_ref, bbr_ref, bbi_ref):
    lr = lr_ref[...]
    li = li_ref[...]
    dt = jnp.exp(ldt_ref[...])
    mag = jnp.exp(lr * dt)
    ab_re = mag * jnp.cos(li * dt)
    ab_im = mag * jnp.sin(li * dt)
    den = lr * lr + li * li
    q_re = ((ab_re - 1.0) * lr + ab_im * li) / den
    q_im = (ab_im * lr - (ab_re - 1.0) * li) / den
    br = br_ref[...]
    bi = bi_ref[...]
    abr_ref[...] = ab_re
    abi_ref[...] = ab_im
    bbr_ref[...] = q_re * br - q_im * bi
    bbi_ref[...] = q_re * bi + q_im * br


def _discretise(lam_re, lam_im, log_dt, b_re, b_im):
    rows = DEPTH * N_SSM_GROUPS
    cols = SSM_STATE * SSM_GROUP
    rep = lambda a: jnp.repeat(a.reshape(rows, SSM_STATE), SSM_GROUP, axis=1)
    full = pl.BlockSpec((rows, cols), lambda: (0, 0))
    out = jax.ShapeDtypeStruct((rows, cols), F32)
    return pl.pallas_call(
        _disc_kernel,
        in_specs=[full, full, pl.BlockSpec((rows, 1), lambda: (0, 0)), full, full],
        out_specs=[full] * 4,
        out_shape=[out] * 4,
        name="s5_discretise",
    )(rep(lam_re), rep(lam_im), log_dt.reshape(rows, 1),
      b_re.reshape(rows, cols), b_im.reshape(rows, cols))


def _block_diag_params(ab_re, ab_im, bb_re, bb_im, c_re, c_im):
    eye = jnp.eye(GROUP_BLOCK, dtype=F32)
    shp = (DEPTH, N_GB, GROUP_BLOCK, SSM_STATE, SSM_GROUP)

    def in_mat(bb):
        m = bb.reshape(shp)[:, :, :, None, :, :] * eye[None, None, :, :, None, None]
        return m.transpose(0, 1, 2, 5, 3, 4).reshape(DEPTH, N_GB, GB_CH, GB_ST)

    def out_mat(cc):
        c5 = cc.reshape(DEPTH, N_GB, GROUP_BLOCK, SSM_GROUP, SSM_STATE)
        m = c5[:, :, :, None, :, :] * eye[None, None, :, :, None, None]
        return m.transpose(0, 1, 2, 5, 3, 4).reshape(DEPTH, N_GB, GB_ST, GB_CH)

    bcat = jnp.concatenate([in_mat(bb_re), in_mat(bb_im)], axis=-1).astype(BF16)
    ccat = jnp.concatenate([out_mat(c_re), -out_mat(c_im)], axis=-2).astype(BF16)
    take = lambda a: a.reshape(DEPTH, N_GB, GROUP_BLOCK, SSM_STATE, SSM_GROUP)[..., 0].reshape(DEPTH, N_GB, 1, GB_ST)
    return bcat, ccat, take(ab_re), take(ab_im)


def _mix_in_kernel(x_ref, sh_ref, sc_ref, g_ref, w_ref, lng_ref, lnb_ref, ua_ref, gub_ref, vn_ref):
    xn = _rms(x_ref[...]) * g_ref[...]
    h = _modulate(xn, sc_ref, sh_ref)
    proj = jnp.dot(h.astype(BF16), w_ref[...], preferred_element_type=F32)
    ua_ref[...] = proj[:, :D_SSM]
    gub_ref[...] = jax.nn.gelu(proj[:, D_SSM:D_SSM + D_GMLP])
    gv = jax.nn.gelu(proj[:, D_SSM + D_GMLP:])
    mu = jnp.mean(gv, axis=-1, keepdims=True)
    var = jnp.mean(jnp.square(gv - mu), axis=-1, keepdims=True)
    vn_ref[...] = (gv - mu) * lax.rsqrt(var + EPS) * lng_ref[...] + lnb_ref[...]


def _mix_in(x, mod, g_mix, w_in_bf, ln_g, ln_b):
    p_tiles = N_P // TM
    mod_spec = lambda m: pl.BlockSpec((1, MOD_ROWS, D_MODEL), lambda i: (i // p_tiles, 0, m))
    out = jax.ShapeDtypeStruct((N_TOK, D_SSM), F32)
    return pl.pallas_call(
        _mix_in_kernel,
        grid=(N_TOK // TM,),
        in_specs=[
            pl.BlockSpec((TM, D_MODEL), lambda i: (i, 0)),
            mod_spec(0), mod_spec(1),
            _const_spec((1, D_MODEL)),
            _const_spec((D_MODEL, D_IN)),
            _const_spec((1, D_GMLP)), _const_spec((1, D_GMLP)),
        ],
        out_specs=[pl.BlockSpec((TM, D_SSM), lambda i: (i, 0))] * 3,
        out_shape=[out] * 3,
        compiler_params=_cparams("arbitrary"),
        name="mix_in",
    )(x, mod, mod, g_mix.reshape(1, D_MODEL), w_in_bf, ln_g.reshape(1, D_GMLP), ln_b.reshape(1, D_GMLP))


def _cmul_add(ar, ai, hr, hi, vr, vi):
    return ar * hr - ai * hi + vr, ar * hi + ai * hr + vi


def _s5_glu(y_s, wglu_ref, bglu_ref, ya_ref):
    ya = jax.nn.gelu(y_s[...])
    z = jnp.dot(ya.astype(BF16), wglu_ref[...], preferred_element_type=F32) + bglu_ref[...]
    ya_ref[...] = ya * jax.nn.sigmoid(z)


def _s5_prompt_kernel(u_ref, bcat_ref, ccat_ref, abr_ref, abi_ref, dsk_ref, wglu_ref, bglu_ref,
                      ya_ref, hre_ref, him_ref, bu_s, hs_s, y_s):
    @pl.when(pl.program_id(0) == 0)
    def _():
        hre_ref[...] = jnp.zeros_like(hre_ref)
        him_ref[...] = jnp.zeros_like(him_ref)

    u = u_ref[...]
    ub = u.astype(BF16)
    first_step = lax.broadcasted_iota(I32, (SUBLANES, SCAN_LANES), 0) < BATCH
    for gb in range(N_GB):
        cols = slice(gb * GB_CH, (gb + 1) * GB_CH)
        bu_s[...] = jnp.dot(ub[:, cols], bcat_ref[gb], preferred_element_type=F32)
        for lh in range(GB_ST // SCAN_LANES):
            re_c = slice(lh * SCAN_LANES, (lh + 1) * SCAN_LANES)
            im_c = slice(GB_ST + lh * SCAN_LANES, GB_ST + (lh + 1) * SCAN_LANES)
            ar = jnp.broadcast_to(abr_ref[gb, :, re_c], (SUBLANES, SCAN_LANES))
            ai = jnp.broadcast_to(abi_ref[gb, :, re_c], (SUBLANES, SCAN_LANES))

            def body(k, carry):
                hr, hi = carry
                rows = pl.ds(pl.multiple_of(k * SUBLANES, SUBLANES), SUBLANES)
                vr = bu_s[rows, re_c]
                vi = bu_s[rows, im_c]
                h1r, h1i = _cmul_add(ar, ai, hr, hi, vr, vi)
                h2r, h2i = _cmul_add(ar, ai, pltpu.roll(h1r, BATCH, 0), pltpu.roll(h1i, BATCH, 0), vr, vi)
                hs_s[rows, re_c] = jnp.where(first_step, h1r, h2r)
                hs_s[rows, im_c] = jnp.where(first_step, h1i, h2i)
                return pltpu.roll(h2r, BATCH, 0), pltpu.roll(h2i, BATCH, 0)

            hr, hi = lax.fori_loop(0, TM // SUBLANES, body, (hre_ref[gb, :, re_c], him_ref[gb, :, re_c]))
            hre_ref[gb, :, re_c] = hr
            him_ref[gb, :, re_c] = hi
        y = jnp.dot(hs_s[...].astype(BF16), ccat_ref[gb], preferred_element_type=F32)
        y_s[:, cols] = y + dsk_ref[:, cols] * u[:, cols]
    _s5_glu(y_s, wglu_ref, bglu_ref, ya_ref)


def _s5_sample_kernel(u_ref, h0r_ref, h0i_ref, bcat_ref, ccat_ref, abr_ref, abi_ref, dsk_ref, wglu_ref,
                      bglu_ref, ya_ref, hre_ref, him_ref, bu_s, hs_s, y_s):
    u = u_ref[...]
    ub = u.astype(BF16)
    for gb in range(N_GB):
        cols = slice(gb * GB_CH, (gb + 1) * GB_CH)
        bu_s[...] = jnp.dot(ub[:, cols], bcat_ref[gb], preferred_element_type=F32)
        for lh in range(GB_ST // SCAN_LANES):
            re_c = slice(lh * SCAN_LANES, (lh + 1) * SCAN_LANES)
            im_c = slice(GB_ST + lh * SCAN_LANES, GB_ST + (lh + 1) * SCAN_LANES)
            st_c = slice(gb * GB_ST + lh * SCAN_LANES, gb * GB_ST + (lh + 1) * SCAN_LANES)
            ar = jnp.broadcast_to(abr_ref[gb, :, re_c], (SUBLANES, SCAN_LANES))
            ai = jnp.broadcast_to(abi_ref[gb, :, re_c], (SUBLANES, SCAN_LANES))

            def body(q, carry):
                r0 = pl.multiple_of(q * SUBLANES, SUBLANES)
                hr = h0r_ref[pl.ds(r0, SUBLANES), st_c]
                hi = h0i_ref[pl.ds(r0, SUBLANES), st_c]
                for t in range(DEC_SEQ):
                    rows = pl.ds(t * DEC_BATCH + r0, SUBLANES)
                    hr, hi = _cmul_add(ar, ai, hr, hi, bu_s[rows, re_c], bu_s[rows, im_c])
                    hs_s[rows, re_c] = hr
                    hs_s[rows, im_c] = hi
                hre_ref[pl.ds(r0, SUBLANES), st_c] = hr
                him_ref[pl.ds(r0, SUBLANES), st_c] = hi
                return carry

            lax.fori_loop(0, DEC_BATCH // SUBLANES, body, 0)
        y = jnp.dot(hs_s[...].astype(BF16), ccat_ref[gb], preferred_element_type=F32)
        y_s[:, cols] = y + dsk_ref[:, cols] * u[:, cols]
    _s5_glu(y_s, wglu_ref, bglu_ref, ya_ref)


def _s5_common_specs():
    return [
        _const_spec((N_GB, GB_CH, 2 * GB_ST)),
        _const_spec((N_GB, 2 * GB_ST, GB_CH)),
        _const_spec((N_GB, 1, GB_ST)), _const_spec((N_GB, 1, GB_ST)),
        _const_spec((1, D_SSM)),
        _const_spec((D_SSM, D_SSM)),
        _const_spec((1, D_SSM)),
    ]


def _s5_prompt(u_a, bcat, ccat, abr, abi, d_skip, w_glu_bf, b_glu):
    st = jax.ShapeDtypeStruct((N_GB, SUBLANES, GB_ST), F32)
    st_spec = pl.BlockSpec((N_GB, SUBLANES, GB_ST), lambda i: (0, 0, 0))
    return pl.pallas_call(
        _s5_prompt_kernel,
        grid=(N_P // TM,),
        in_specs=[pl.BlockSpec((TM, D_SSM), lambda i: (i, 0))] + _s5_common_specs(),
        out_specs=[pl.BlockSpec((TM, D_SSM), lambda i: (i, 0)), st_spec, st_spec],
        out_shape=[jax.ShapeDtypeStruct((N_P, D_SSM), F32), st, st],
        scratch_shapes=[pltpu.VMEM((TM, 2 * GB_ST), F32), pltpu.VMEM((TM, 2 * GB_ST), F32),
                        pltpu.VMEM((TM, D_SSM), F32)],
        compiler_params=_cparams("arbitrary"),
        name="s5_prompt",
    )(u_a, bcat, ccat, abr, abi, d_skip.reshape(1, D_SSM), w_glu_bf, b_glu.reshape(1, D_SSM))


def _s5_sample(u_a, h0_re, h0_im, bcat, ccat, abr, abi, d_skip, w_glu_bf, b_glu):
    st = jax.ShapeDtypeStruct((DEC_BATCH, N_SSM_GROUPS * SSM_STATE), F32)
    st_spec = pl.BlockSpec((DEC_BATCH, N_SSM_GROUPS * SSM_STATE), lambda i: (0, 0))
    return pl.pallas_call(
        _s5_sample_kernel,
        grid=(1,),
        in_specs=[pl.BlockSpec((N_S, D_SSM), lambda i: (N_P // N_S, 0)), st_spec, st_spec] + _s5_common_specs(),
        out_specs=[pl.BlockSpec((N_S, D_SSM), lambda i: (0, 0)), st_spec, st_spec],
        out_shape=[jax.ShapeDtypeStruct((N_S, D_SSM), F32), st, st],
        scratch_shapes=[pltpu.VMEM((N_S, 2 * GB_ST), F32), pltpu.VMEM((N_S, 2 * GB_ST), F32),
                        pltpu.VMEM((N_S, D_SSM), F32)],
        compiler_params=_cparams("arbitrary"),
        name="s5_sample",
    )(u_a, h0_re, h0_im, bcat, ccat, abr, abi, d_skip.reshape(1, D_SSM), w_glu_bf, b_glu.reshape(1, D_SSM))


def _gmlp_kernel(gub_ref, vn_ref, wk_ref, bsr_ref, ws4_ref, bs4_ref, yb_ref):
    i = pl.program_id(0)

    @pl.when(i < N_P // TM_MIX)
    def _prompt_chunk():
        vb = vn_ref[...].astype(BF16)
        row = lax.broadcasted_iota(I32, (TM_MIX, TM_MIX), 0)
        col = lax.broadcasted_iota(I32, (TM_MIX, TM_MIX), 1)
        for h in range(N_GMLP_HEADS):
            hc = slice(h * GMLP_HEAD, (h + 1) * GMLP_HEAD)
            w = jnp.where(row >= col, wk_ref[h], jnp.zeros((), BF16))
            mix = jnp.dot(w, vb[:, hc], preferred_element_type=F32) + bsr_ref[:, h:h + 1]
            yb_ref[:, hc] = gub_ref[:, hc] * mix

    @pl.when(i == N_P // TM_MIX)
    def _sample_chunk():
        for t in range(DEC_SEQ):
            acc = jnp.broadcast_to(bs4_ref[t:t + 1, :], (DEC_BATCH, D_GMLP))
            for s in range(t + 1):
                w_ts = ws4_ref[t * DEC_SEQ + s:t * DEC_SEQ + s + 1, :]
                acc = acc + w_ts * vn_ref[s * DEC_BATCH:(s + 1) * DEC_BATCH, :]
            rows = slice(t * DEC_BATCH, (t + 1) * DEC_BATCH)
            yb_ref[rows, :] = gub_ref[rows, :] * acc


def _gmlp(gu_b, vn, wk, bsr, ws4, bs4):
    tok = pl.BlockSpec((TM_MIX, D_GMLP), lambda i: (i, 0))
    return pl.pallas_call(
        _gmlp_kernel,
        grid=(N_TOK // TM_MIX,),
        in_specs=[
            tok, tok,
            _const_spec((N_GMLP_HEADS, TM_MIX, TM_MIX)),
            _const_spec((TM_MIX, N_GMLP_HEADS)),
            _const_spec((DEC_SEQ * DEC_SEQ, D_GMLP)),
            _const_spec((DEC_SEQ, D_GMLP)),
        ],
        out_specs=tok,
        out_shape=jax.ShapeDtypeStruct((N_TOK, D_GMLP), F32),
        compiler_params=_cparams("arbitrary"),
        name="gmlp_mix",
    )(gu_b, vn, wk, bsr, ws4, bs4)


def _mix_out_kernel(x_ref, ya_ref, yb_ref, goa_ref, gob_ref, wout_ref, gm_ref, shf_ref, scf_ref, gffn_ref,
                    wr_ref, br_ref, xnew_ref, h2_ref, idx_ref, gate_ref):
    na = (_rms(ya_ref[...]) * goa_ref[...]).astype(BF16)
    nb = (_rms(yb_ref[...]) * gob_ref[...]).astype(BF16)
    o = (jnp.dot(na, wout_ref[:D_SSM, :], preferred_element_type=F32)
         + jnp.dot(nb, wout_ref[D_SSM:, :], preferred_element_type=F32))
    x_new = x_ref[...] + _gate(o, gm_ref)
    xnew_ref[...] = x_new

    h2 = _modulate(_rms(x_new) * gffn_ref[...], scf_ref, shf_ref)
    for lt in range(ROW_TILES):
        h2_ref[pl.ds(lt, TM, stride=ROW_TILES), :] = h2[:, lt * LANES:(lt + 1) * LANES]

    logits = jnp.dot(h2, wr_ref[...], preferred_element_type=F32, precision=lax.Precision.HIGHEST) + br_ref[...]
    lane = lax.broadcasted_iota(I32, (TM, LANES), 1)
    work = logits
    idx_out = jnp.zeros((TM, LANES), I32)
    exp_out = jnp.zeros((TM, LANES), F32)
    denom = jnp.zeros((TM, 1), F32)
    top = None
    for k in range(TOP_K):
        m = jnp.max(work, axis=-1, keepdims=True)
        sel = jnp.min(jnp.where(work == m, lane, LANES), axis=-1, keepdims=True)
        top = m if k == 0 else top
        e = jnp.exp(m - top)
        denom = denom + e
        idx_out = jnp.where(lane == k, sel, idx_out)
        exp_out = jnp.where(lane == k, e, exp_out)
        work = jnp.where(lane == sel, -jnp.inf, work)
    idx_ref[...] = idx_out
    gate_ref[...] = exp_out / denom


def _mix_out(x, y_a, y_b, g_out_a, g_out_b, w_out_bf, mod, g_ffn, wr_pad, br_pad):
    p_tiles = N_P // TM
    tok = lambda w: pl.BlockSpec((TM, w), lambda i: (i, 0))
    mod_spec = lambda m: pl.BlockSpec((1, MOD_ROWS, D_MODEL), lambda i: (i // p_tiles, 0, m))
    return pl.pallas_call(
        _mix_out_kernel,
        grid=(N_TOK // TM,),
        in_specs=[
            tok(D_MODEL), tok(D_SSM), tok(D_GMLP),
            _const_spec((1, D_SSM)), _const_spec((1, D_GMLP)),
            _const_spec((D_MODEL, D_MODEL)),
            mod_spec(2), mod_spec(3), mod_spec(4),
            _const_spec((1, D_MODEL)),
            _const_spec((D_MODEL, LANES)), _const_spec((1, LANES)),
        ],
        out_specs=[
            tok(D_MODEL),
            pl.BlockSpec((TM * ROW_TILES, LANES), lambda i: (i, 0)),
            tok(LANES), tok(LANES),
        ],
        out_shape=[
            jax.ShapeDtypeStruct((N_TOK, D_MODEL), F32),
            jax.ShapeDtypeStruct((N_TOK * ROW_TILES, LANES), F32),
            jax.ShapeDtypeStruct((N_TOK, LANES), I32),
            jax.ShapeDtypeStruct((N_TOK, LANES), F32),
        ],
        compiler_params=_cparams("arbitrary"),
        name="mix_out_router",
    )(x, y_a, y_b, g_out_a.reshape(1, D_SSM), g_out_b.reshape(1, D_GMLP),
      w_out_bf, mod, mod, mod, g_ffn.reshape(1, D_MODEL), wr_pad, br_pad)


def _route(idx):
    flat_e = idx.reshape(-1)
    onehot = (flat_e[:, None] == jnp.arange(N_EXPERTS, dtype=I32)[None, :]).astype(I32)
    csum = jnp.cumsum(onehot, axis=0)
    rank = jnp.take_along_axis(csum, flat_e[:, None], axis=1)[:, 0] - 1
    counts = csum[-1]
    ntiles = (counts + ETM - 1) // ETM
    tile_end = jnp.cumsum(ntiles)
    tile0 = tile_end - ntiles
    dest = tile0[flat_e] * ETM + rank
    n_used = tile_end[-1]
    t = jnp.arange(N_ETILES, dtype=I32)
    tb = jnp.minimum(t, n_used - 1)
    te = jnp.minimum(jnp.searchsorted(tile_end, tb, side="right").astype(I32), N_EXPERTS - 1)
    nv = jnp.where(t < n_used, jnp.clip(counts[te] - (t - tile0[te]) * ETM, 0, ETM), 0).astype(I32)
    full = counts // ESB * ESB
    zrow = jnp.where(counts % ESB != 0, tile0 * ETM + full, -1).astype(I32)
    return dest.astype(I32), te, nv, tb, zrow


def _row_start(row):
    return row * ROW_TILES if isinstance(row, int) else pl.multiple_of(row * ROW_TILES, ROW_TILES)


def _row_copy(src_hbm, src_row, dst_hbm, dst_row, n_rows, sem):
    return pltpu.make_async_copy(
        src_hbm.at[pl.ds(_row_start(src_row), n_rows * ROW_TILES)],
        dst_hbm.at[pl.ds(_row_start(dst_row), n_rows * ROW_TILES)],
        sem)


def _gather_kernel(zrow_ref, dest_ref, h2_hbm, zeros_hbm, xs_hbm, sem_z, sem):
    i = pl.program_id(0)

    @pl.when(i == 0)
    def _zero_partial_subblocks():
        for e in range(N_EXPERTS):
            @pl.when(zrow_ref[e] >= 0)
            def _():
                _row_copy(zeros_hbm, 0, xs_hbm, zrow_ref[e], ESB, sem_z).start()
        for e in range(N_EXPERTS):
            @pl.when(zrow_ref[e] >= 0)
            def _():
                _row_copy(zeros_hbm, 0, xs_hbm, zrow_ref[e], ESB, sem_z).wait()

    def body(a, carry):
        tok = (i * GATHER_BLK + a) // TOP_K
        _row_copy(h2_hbm, tok, xs_hbm, dest_ref[0, 0, a], 1, sem).start()
        return carry

    lax.fori_loop(0, GATHER_BLK, body, 0, unroll=8)
    _row_copy(h2_hbm, 0, xs_hbm, 0, GATHER_BLK, sem).wait()


def _gather_rows(dest, zrow, h2_rows):
    zeros = jnp.zeros((ESB * ROW_TILES, LANES), F32)
    n_blk = N_ASSIGN // GATHER_BLK
    return pl.pallas_call(
        _gather_kernel,
        grid_spec=pltpu.PrefetchScalarGridSpec(
            num_scalar_prefetch=1,
            grid=(n_blk,),
            in_specs=[
                pl.BlockSpec((1, 1, GATHER_BLK), lambda i, z: (i, 0, 0), memory_space=pltpu.SMEM),
                pl.BlockSpec(memory_space=pl.ANY),
                pl.BlockSpec(memory_space=pl.ANY),
            ],
            out_specs=pl.BlockSpec(memory_space=pl.ANY),
            scratch_shapes=[pltpu.SemaphoreType.DMA(()), pltpu.SemaphoreType.DMA(())],
        ),
        out_shape=jax.ShapeDtypeStruct((E_ROWS * ROW_TILES, LANES), F32),
        compiler_params=_cparams("arbitrary"),
        name="moe_gather",
    )(zrow, dest.reshape(n_blk, 1, GATHER_BLK), h2_rows, zeros)


def _expert1_kernel(te_ref, nv_ref, tb_ref, x_ref, wg_ref, wl_ref, bg_ref, bl_ref, h_ref, xb_s):
    t = pl.program_id(0)
    j = pl.program_id(1)
    n = nv_ref[t]

    @pl.when(jnp.logical_and(j == 0, n > 0))
    def _to_matmul_layout():
        for sb in range(N_ESB):
            @pl.when(sb * ESB < n)
            def _():
                for lt in range(ROW_TILES):
                    v = x_ref[pl.ds(sb * ESB * ROW_TILES + lt, ESB, stride=ROW_TILES), :]
                    xb_s[sb * ESB:(sb + 1) * ESB, lt * LANES:(lt + 1) * LANES] = v.astype(BF16)

    @pl.when(n > 0)
    def _compute():
        wg = wg_ref[0].astype(BF16)
        wl = wl_ref[0].astype(BF16)
        for sb in range(N_ESB):
            @pl.when(sb * ESB < n)
            def _():
                xs = xb_s[sb * ESB:(sb + 1) * ESB, :]
                glu = jnp.dot(xs, wg, preferred_element_type=F32) + bg_ref[0]
                lin = jnp.dot(xs, wl, preferred_element_type=F32) + bl_ref[0]
                glu = jnp.minimum(glu, SWIGLU_LIMIT)
                lin = jnp.clip(lin, -SWIGLU_LIMIT, SWIGLU_LIMIT)
                h_ref[sb * ESB:(sb + 1) * ESB, :] = (glu * jax.nn.sigmoid(SWIGLU_ALPHA * glu) * (lin + 1.0)).astype(BF16)


def _expert2_kernel(te_ref, nv_ref, tb_ref, h_ref, w_ref, b_ref, y_ref):
    t = pl.program_id(0)
    j = pl.program_id(1)
    n = nv_ref[t]

    @pl.when(n > 0)
    def _compute():
        w = w_ref[0].astype(BF16)
        for sb in range(N_ESB):
            @pl.when(sb * ESB < n)
            def _():
                acc = jnp.dot(h_ref[sb * ESB:(sb + 1) * ESB, :], w, preferred_element_type=F32) + b_ref[0]
                for q in range(TN2 // LANES):
                    start = sb * ESB * ROW_TILES + j * (TN2 // LANES) + q
                    y_ref[pl.ds(start, ESB, stride=ROW_TILES), :] = acc[:, q * LANES:(q + 1) * LANES]


def _step(nv, t, j, last):
    return jnp.where(nv[t] > 0, j, last)


def _experts(te, nv, tb, xs_rows, w1, b1, w2, b2):
    h = pl.pallas_call(
        _expert1_kernel,
        grid_spec=pltpu.PrefetchScalarGridSpec(
            num_scalar_prefetch=3,
            grid=(N_ETILES, NJ1),
            in_specs=[
                pl.BlockSpec((ETM * ROW_TILES, LANES), lambda t, j, te, nv, tb: (tb[t], 0)),
                pl.BlockSpec((1, D_MODEL, TF), lambda t, j, te, nv, tb: (te[t], 0, _step(nv, t, j, NJ1 - 1))),
                pl.BlockSpec((1, D_MODEL, TF), lambda t, j, te, nv, tb: (te[t], 0, NJ1 + _step(nv, t, j, NJ1 - 1))),
                pl.BlockSpec((1, 1, TF), lambda t, j, te, nv, tb: (te[t], 0, _step(nv, t, j, NJ1 - 1))),
                pl.BlockSpec((1, 1, TF), lambda t, j, te, nv, tb: (te[t], 0, NJ1 + _step(nv, t, j, NJ1 - 1))),
            ],
            out_specs=pl.BlockSpec((ETM, TF), lambda t, j, te, nv, tb: (tb[t], _step(nv, t, j, NJ1 - 1))),
            scratch_shapes=[pltpu.VMEM((ETM, D_MODEL), BF16)],
        ),
        out_shape=jax.ShapeDtypeStruct((E_ROWS, D_FF), BF16),
        compiler_params=_cparams("arbitrary", "arbitrary"),
        name="moe_expert_in",
    )(te, nv, tb, xs_rows, w1, w1, b1.reshape(N_EXPERTS, 1, 2 * D_FF), b1.reshape(N_EXPERTS, 1, 2 * D_FF))
    return pl.pallas_call(
        _expert2_kernel,
        grid_spec=pltpu.PrefetchScalarGridSpec(
            num_scalar_prefetch=3,
            grid=(N_ETILES, NJ2),
            in_specs=[
                pl.BlockSpec((ETM, D_FF), lambda t, j, te, nv, tb: (tb[t], 0)),
                pl.BlockSpec((1, D_FF, TN2), lambda t, j, te, nv, tb: (te[t], 0, _step(nv, t, j, NJ2 - 1))),
                pl.BlockSpec((1, 1, TN2), lambda t, j, te, nv, tb: (te[t], 0, _step(nv, t, j, NJ2 - 1))),
            ],
            out_specs=pl.BlockSpec((ETM * ROW_TILES, LANES), lambda t, j, te, nv, tb: (tb[t], 0)),
        ),
        out_shape=jax.ShapeDtypeStruct((E_ROWS * ROW_TILES, LANES), F32),
        compiler_params=_cparams("arbitrary", "arbitrary"),
        name="moe_expert_out",
    )(te, nv, tb, h, w2, b2.reshape(N_EXPERTS, 1, D_MODEL))


N_COMB = TM_COMB * TOP_K


def _combine_kernel(final, dcur_ref, dnext_ref, y_hbm, x_ref, gate_ref, gf_ref, gfin_ref, o_ref, buf, sem):
    i = pl.program_id(0)
    n = pl.num_programs(0)
    slot = i % 2

    def issue(d_ref, s):
        def body(a, carry):
            tl = a // TOP_K
            k = a % TOP_K
            src = d_ref[0, 0, a]
            pltpu.make_async_copy(
                y_hbm.at[pl.ds(pl.multiple_of(src * ROW_TILES, ROW_TILES), ROW_TILES)],
                buf.at[s, pl.ds(pl.multiple_of((k * TM_COMB + tl) * ROW_TILES, ROW_TILES), ROW_TILES)],
                sem.at[s]).start()
            return carry
        lax.fori_loop(0, N_COMB, body, 0, unroll=8)

    @pl.when(i == 0)
    def _():
        issue(dcur_ref, 0)

    @pl.when(i + 1 < n)
    def _():
        issue(dnext_ref, 1 - slot)

    pltpu.make_async_copy(buf.at[slot], buf.at[slot], sem.at[slot]).wait()

    gates = gate_ref[...]
    x = x_ref[...]
    gf = gf_ref[0]
    pieces = []
    for lt in range(ROW_TILES):
        acc = jnp.zeros((TM_COMB, LANES), F32)
        for k in range(TOP_K):
            v = buf[slot, pl.ds(k * TM_COMB * ROW_TILES + lt, TM_COMB, stride=ROW_TILES), :]
            acc = acc + gates[:, k:k + 1] * v
        lc = slice(lt * LANES, (lt + 1) * LANES)
        pieces.append(x[:, lc] + gf[:, lc] * acc)
    out = jnp.concatenate(pieces, axis=-1)
    if final:
        out = _rms(out) * gfin_ref[...]
    o_ref[...] = out


def _combine(dest, y_rows, x_new, gates, mod, g_final, final):
    n_blk = N_TOK // TM_COMB
    p_tiles = N_P // TM_COMB
    d3 = dest.reshape(n_blk, 1, N_COMB)
    return pl.pallas_call(
        functools.partial(_combine_kernel, final),
        grid=(n_blk,),
        in_specs=[
            pl.BlockSpec((1, 1, N_COMB), lambda i: (i, 0, 0), memory_space=pltpu.SMEM),
            pl.BlockSpec((1, 1, N_COMB), lambda i: (jnp.minimum(i + 1, n_blk - 1), 0, 0), memory_space=pltpu.SMEM),
            pl.BlockSpec(memory_space=pl.ANY),
            pl.BlockSpec((TM_COMB, D_MODEL), lambda i: (i, 0)),
            pl.BlockSpec((TM_COMB, LANES), lambda i: (i, 0)),
            pl.BlockSpec((1, MOD_ROWS, D_MODEL), lambda i: (i // p_tiles, 0, 5)),
            _const_spec((1, D_MODEL)),
        ],
        out_specs=pl.BlockSpec((TM_COMB, D_MODEL), lambda i: (i, 0)),
        out_shape=jax.ShapeDtypeStruct((N_TOK, D_MODEL), F32),
        scratch_shapes=[pltpu.VMEM((2, N_COMB * ROW_TILES, LANES), F32), pltpu.SemaphoreType.DMA((2,))],
        compiler_params=_cparams("arbitrary"),
        name="moe_combine",
    )(d3, d3, y_rows, x_new, gates, mod, g_final.reshape(1, D_MODEL))


def kernel(x_prompt, x_sample, c_prompt, c_sample, state_ssm_re, state_ssm_im, g_mix, w_ada, b_ada, w_in, lam_re, lam_im, log_dt, b_re, b_im, c_re, c_im, d_skip, w_glu, b_glu, ln_v_g, ln_v_b, w_s, b_s, g_out_a, g_out_b, w_out, g_ffn, w_router, b_router, w1, b1, w2, b2, g_final):
    x = jnp.concatenate([
        x_prompt.transpose(1, 0, 2).reshape(N_P, D_MODEL),
        x_sample.transpose(1, 0, 2).reshape(N_S, D_MODEL)], axis=0)

    c_all = jnp.concatenate([c_prompt, jnp.zeros((8 - BATCH, D_MODEL), F32), c_sample], axis=0)
    mod_all = _ada_mod(c_all, w_ada, b_ada)
    mod_tok = jnp.stack([jnp.tile(mod_all[:, :BATCH], (1, MOD_ROWS // BATCH, 1)), mod_all[:, 8:]], axis=1)

    ab_re, ab_im, bb_re, bb_im = _discretise(lam_re, lam_im, log_dt, b_re, b_im)
    bcat, ccat, abr, abi = _block_diag_params(ab_re, ab_im, bb_re, bb_im, c_re, c_im)

    eye_b = jnp.eye(BATCH, dtype=F32)
    wk = (w_s[:, :, :, None, :, None] * eye_b[None, None, None, :, None, :]).reshape(
        DEPTH, N_GMLP_HEADS, TM_MIX, TM_MIX).astype(BF16)
    bsr = jnp.repeat(b_s.transpose(0, 2, 1), BATCH, axis=1)
    ws4 = jnp.repeat(w_s[:, :, :DEC_SEQ, :DEC_SEQ].transpose(0, 2, 3, 1).reshape(
        DEPTH, DEC_SEQ * DEC_SEQ, N_GMLP_HEADS), GMLP_HEAD, axis=2)
    bs4 = jnp.repeat(b_s[:, :, :DEC_SEQ].transpose(0, 2, 1), GMLP_HEAD, axis=2)
    wr_pad = jnp.pad(w_router, ((0, 0), (0, 0), (0, LANES - N_EXPERTS)))
    br_pad = jnp.pad(b_router, ((0, 0), (0, LANES - N_EXPERTS)), constant_values=-1e30).reshape(DEPTH, 1, LANES)

    h0_re = state_ssm_re.reshape(DEPTH, DEC_BATCH, N_SSM_GROUPS * SSM_STATE)
    h0_im = state_ssm_im.reshape(DEPTH, DEC_BATCH, N_SSM_GROUPS * SSM_STATE)

    st_p_re, st_p_im, st_s_re, st_s_im, v_rows = [], [], [], [], []
    for l in range(DEPTH):
        mod = mod_tok[l]
        u_a, gu_b, vn = _mix_in(x, mod, g_mix[l], w_in[l].astype(BF16), ln_v_g[l], ln_v_b[l])
        wglu_bf = w_glu[l].astype(BF16)
        ya_p, hp_re, hp_im = _s5_prompt(u_a, bcat[l], ccat[l], abr[l], abi[l], d_skip[l], wglu_bf, b_glu[l])
        ya_s, hs_re, hs_im = _s5_sample(u_a, h0_re[l], h0_im[l], bcat[l], ccat[l], abr[l], abi[l], d_skip[l],
                                        wglu_bf, b_glu[l])
        y_a = jnp.concatenate([ya_p, ya_s], axis=0)
        y_b = _gmlp(gu_b, vn, wk[l], bsr[l], ws4[l], bs4[l])
        x_new, h2_rows, idx, gates = _mix_out(
            x, y_a, y_b, g_out_a[l], g_out_b[l], w_out[l].astype(BF16), mod, g_ffn[l], wr_pad[l], br_pad[l])
        dest, te, nv, tb, zrow = _route(idx[:, :TOP_K])
        xs_rows = _gather_rows(dest, zrow, h2_rows)
        y_rows = _experts(te, nv, tb, xs_rows, w1[l], b1[l], w2[l], b2[l])
        x = _combine(dest, y_rows, x_new, gates, mod, g_final, l == DEPTH - 1)

        unblock = lambda h: h[:, :BATCH].transpose(1, 0, 2).reshape(BATCH, N_SSM_GROUPS, SSM_STATE)
        st_p_re.append(unblock(hp_re))
        st_p_im.append(unblock(hp_im))
        st_s_re.append(hs_re.reshape(DEC_BATCH, N_SSM_GROUPS, SSM_STATE))
        st_s_im.append(hs_im.reshape(DEC_BATCH, N_SSM_GROUPS, SSM_STATE))
        v_rows.append(vn[N_P:].reshape(DEC_SEQ, DEC_BATCH, D_GMLP).transpose(1, 0, 2))

    y_prompt = x[:N_P].reshape(SEQ, BATCH, D_MODEL).transpose(1, 0, 2)
    y_sample = x[N_P:].reshape(DEC_SEQ, DEC_BATCH, D_MODEL).transpose(1, 0, 2)
    return (y_prompt, y_sample, jnp.stack(st_p_re), jnp.stack(st_p_im),
            jnp.stack(st_s_re), jnp.stack(st_s_im), jnp.stack(v_rows))
```

```python
import functools

import jax
import jax.numpy as jnp
from jax import lax
from jax.experimental import pallas as pl
from jax.experimental.pallas import tpu as pltpu

F32 = jnp.float32
BF16 = jnp.bfloat16
I32 = jnp.int32

D_MODEL = 2048
BATCH = 4
SEQ = 2048
DEPTH = 4
DEC_BATCH = 128
DEC_SEQ = 4
D_SSM = 1024
SSM_GROUP = 16
N_SSM_GROUPS = 64
SSM_STATE = 64
D_GMLP = 1024
GMLP_HEAD = 128
N_GMLP_HEADS = 8
CHUNK = 128
D_IN = D_SSM + 2 * D_GMLP
N_EXPERTS = 32
TOP_K = 4
D_FF = D_MODEL
SWIGLU_ALPHA = 1.702
SWIGLU_LIMIT = 7.0
N_MOD = 6
EPS = 1e-5

LANES = 128
SUBLANES = 8
N_P = BATCH * SEQ
N_S = DEC_BATCH * DEC_SEQ
N_TOK = N_P + N_S
N_ASSIGN = N_TOK * TOP_K
C_ROWS = 8 + DEC_BATCH
MOD_ROWS = 128

TM = 256
TM_S5 = 512
TM_MIX = CHUNK * BATCH
TM_COMB = 128
ROW_TILES = D_MODEL // LANES
ROW_PITCH = ROW_TILES + SUBLANES

GROUP_BLOCK = 16
N_GB = N_SSM_GROUPS // GROUP_BLOCK
GB_CH = GROUP_BLOCK * SSM_GROUP
GB_ST = GROUP_BLOCK * SSM_STATE
SCAN_LANES = 512

ESB = 256
N_ESB = 5
ETM = N_ESB * ESB
N_ETILES = N_ASSIGN // ETM + N_EXPERTS
E_ROWS = N_ETILES * ETM
TF = 512
NJ1 = D_FF // TF
TN2 = 512
NJ2 = D_MODEL // TN2
N_COMB = TM_COMB * TOP_K

VMEM_LIMIT = 56 * 1024 * 1024


def _cparams(*sem):
    return pltpu.CompilerParams(dimension_semantics=sem, vmem_limit_bytes=VMEM_LIMIT)


def _const_spec(shape):
    zeros = (0,) * len(shape)
    return pl.BlockSpec(shape, lambda *_: zeros, pipeline_mode=pl.Buffered(1))


def _rms(x):
    return x * lax.rsqrt(jnp.mean(x * x, axis=-1, keepdims=True) + EPS)


def _modulate(xn, sc_ref, sh_ref):
    rows, dim = xn.shape
    x3 = xn.reshape(rows // MOD_ROWS, MOD_ROWS, dim)
    return (x3 * (1.0 + sc_ref[0][None]) + sh_ref[0][None]).reshape(rows, dim)


def _gate(v, g_ref):
    rows, dim = v.shape
    return (v.reshape(rows // MOD_ROWS, MOD_ROWS, dim) * g_ref[0][None]).reshape(rows, dim)


def _ada_kernel(c_ref, w_ref, b_ref, o_ref):
    s = jax.nn.silu(c_ref[...]).astype(BF16)
    o_ref[0] = jnp.dot(s, w_ref[0].astype(BF16), preferred_element_type=F32) + b_ref[0]


def _ada_mod(c_all, w_ada, b_ada):
    tn = 1024
    nj = N_MOD * D_MODEL // tn
    return pl.pallas_call(
        _ada_kernel,
        grid=(DEPTH, nj),
        in_specs=[
            pl.BlockSpec((C_ROWS, D_MODEL), lambda l, j: (0, 0)),
            pl.BlockSpec((1, D_MODEL, tn), lambda l, j: (l, 0, j)),
            pl.BlockSpec((1, 1, tn), lambda l, j: (l, 0, j)),
        ],
        out_specs=pl.BlockSpec((1, C_ROWS, tn), lambda l, j: (l, 0, j)),
        out_shape=jax.ShapeDtypeStruct((DEPTH, C_ROWS, N_MOD * D_MODEL), F32),
        compiler_params=_cparams("arbitrary", "arbitrary"),
        name="ada_mod",
    )(c_all, w_ada, b_ada.reshape(DEPTH, 1, N_MOD * D_MODEL))


def _disc_kernel(lr_ref, li_ref, ldt_ref, br_ref, bi_ref, abr_ref, abi_ref, bbr_ref, bbi_ref):
    lr = lr_ref[...]
    li = li_ref[...]
    dt = jnp.exp(ldt_ref[...])
    mag = jnp.exp(lr * dt)
    ab_re = mag * jnp.cos(li * dt)
    ab_im = mag * jnp.sin(li * dt)
    den = lr * lr + li * li
    q_re = ((ab_re - 1.0) * lr + ab_im * li) / den
    q_im = (ab_im * lr - (ab_re - 1.0) * li) / den
    br = br_ref[...]
    bi = bi_ref[...]
    abr_ref[...] = ab_re
    abi_ref[...] = ab_im
    bbr_ref[...] = q_re * br - q_im * bi
    bbi_ref[...] = q_re * bi + q_im * br


def _discretise(lam_re, lam_im, log_dt, b_re, b_im):
    rows = DEPTH * N_SSM_GROUPS
    cols = SSM_STATE * SSM_GROUP
    rep = lambda a: jnp.repeat(a.reshape(rows, SSM_STATE), SSM_GROUP, axis=1)
    full = pl.BlockSpec((rows, cols), lambda: (0, 0))
    out = jax.ShapeDtypeStruct((rows, cols), F32)
    return pl.pallas_call(
        _disc_kernel,
        in_specs=[full, full, pl.BlockSpec((rows, 1), lambda: (0, 0)), full, full],
        out_specs=[full] * 4,
        out_shape=[out] * 4,
        name="s5_discretise",
    )(rep(lam_re), rep(lam_im), log_dt.reshape(rows, 1),
      b_re.reshape(rows, cols), b_im.reshape(rows, cols))


def _block_diag_params(ab_re, ab_im, bb_re, bb_im, c_re, c_im):
    eye = jnp.eye(GROUP_BLOCK, dtype=F32)
    shp = (DEPTH, N_GB, GROUP_BLOCK, SSM_STATE, SSM_GROUP)

    def in_mat(bb):
        m = bb.reshape(shp)[:, :, :, None, :, :] * eye[None, None, :, :, None, None]
        return m.transpose(0, 1, 2, 5, 3, 4).reshape(DEPTH, N_GB, GB_CH, GB_ST)

    def out_mat(cc):
        c5 = cc.reshape(DEPTH, N_GB, GROUP_BLOCK, SSM_GROUP, SSM_STATE)
        m = c5[:, :, :, None, :, :] * eye[None, None, :, :, None, None]
        return m.transpose(0, 1, 2, 5, 3, 4).reshape(DEPTH, N_GB, GB_ST, GB_CH)

    bcat = jnp.concatenate([in_mat(bb_re), in_mat(bb_im)], axis=-1).astype(BF16)
    ccat = jnp.concatenate([out_mat(c_re), -out_mat(c_im)], axis=-2).astype(BF16)
    take = lambda a: a.reshape(DEPTH, N_GB, GROUP_BLOCK, SSM_STATE, SSM_GROUP)[..., 0].reshape(DEPTH, N_GB, 1, GB_ST)
    return bcat, ccat, take(ab_re), take(ab_im)


def _mix_in_kernel(x_ref, sh_ref, sc_ref, g_ref, w_ref, lng_ref, lnb_ref, ua_ref, gub_ref, vn_ref):
    xn = _rms(x_ref[...]) * g_ref[...]
    h = _modulate(xn, sc_ref, sh_ref)
    proj = jnp.dot(h.astype(BF16), w_ref[...], preferred_element_type=F32)
    ua_ref[...] = proj[:, :D_SSM]
    gub_ref[...] = jax.nn.gelu(proj[:, D_SSM:D_SSM + D_GMLP])
    gv = jax.nn.gelu(proj[:, D_SSM + D_GMLP:])
    mu = jnp.mean(gv, axis=-1, keepdims=True)
    var = jnp.mean(jnp.square(gv - mu), axis=-1, keepdims=True)
    vn_ref[...] = (gv - mu) * lax.rsqrt(var + EPS) * lng_ref[...] + lnb_ref[...]


def _mix_in(x, mod, g_mix, w_in_bf, ln_g, ln_b):
    p_tiles = N_P // TM
    mod_spec = lambda m: pl.BlockSpec((1, MOD_ROWS, D_MODEL), lambda i: (i // p_tiles, 0, m))
    out = jax.ShapeDtypeStruct((N_TOK, D_SSM), F32)
    return pl.pallas_call(
        _mix_in_kernel,
        grid=(N_TOK // TM,),
        in_specs=[
            pl.BlockSpec((TM, D_MODEL), lambda i: (i, 0)),
            mod_spec(0), mod_spec(1),
            _const_spec((1, D_MODEL)),
            _const_spec((D_MODEL, D_IN)),
            _const_spec((1, D_GMLP)), _const_spec((1, D_GMLP)),
        ],
        out_specs=[pl.BlockSpec((TM, D_SSM), lambda i: (i, 0))] * 3,
        out_shape=[out] * 3,
        compiler_params=_cparams("arbitrary"),
        name="mix_in",
    )(x, mod, mod, g_mix.reshape(1, D_MODEL), w_in_bf, ln_g.reshape(1, D_GMLP), ln_b.reshape(1, D_GMLP))


def _cmul_add(ar, ai, hr, hi, vr, vi):
    return ar * hr - ai * hi + vr, ar * hi + ai * hr + vi


def _s5_glu(y_s, wglu_ref, bglu_ref, ya_ref):
    ya = jax.nn.gelu(y_s[...])
    z = jnp.dot(ya.astype(BF16), wglu_ref[...], preferred_element_type=F32) + bglu_ref[...]
    ya_ref[...] = ya * jax.nn.sigmoid(z)


def _s5_prompt_kernel(u_ref, bcat_ref, ccat_ref, abr_ref, abi_ref, dsk_ref, wglu_ref, bglu_ref,
                      ya_ref, hre_ref, him_ref, bu_s, hs_s, y_s):
    @pl.when(pl.program_id(0) == 0)
    def _():
        hre_ref[...] = jnp.zeros_like(hre_ref)
        him_ref[...] = jnp.zeros_like(him_ref)

    u = u_ref[...]
    ub = u.astype(BF16)
    first_step = lax.broadcasted_iota(I32, (SUBLANES, SCAN_LANES), 0) < BATCH
    for gb in range(N_GB):
        cols = slice(gb * GB_CH, (gb + 1) * GB_CH)
        bu_s[...] = jnp.dot(ub[:, cols], bcat_ref[gb], preferred_element_type=F32)
        for lh in range(GB_ST // SCAN_LANES):
            re_c = slice(lh * SCAN_LANES, (lh + 1) * SCAN_LANES)
            im_c = slice(GB_ST + lh * SCAN_LANES, GB_ST + (lh + 1) * SCAN_LANES)
            ar = jnp.broadcast_to(abr_ref[gb, :, re_c], (SUBLANES, SCAN_LANES))
            ai = jnp.broadcast_to(abi_ref[gb, :, re_c], (SUBLANES, SCAN_LANES))

            def body(k, carry):
                hr, hi = carry
                rows = pl.ds(pl.multiple_of(k * SUBLANES, SUBLANES), SUBLANES)
                vr = bu_s[rows, re_c]
                vi = bu_s[rows, im_c]
                h1r, h1i = _cmul_add(ar, ai, hr, hi, vr, vi)
                h2r, h2i = _cmul_add(ar, ai, pltpu.roll(h1r, BATCH, 0), pltpu.roll(h1i, BATCH, 0), vr, vi)
                hs_s[rows, re_c] = jnp.where(first_step, h1r, h2r)
                hs_s[rows, im_c] = jnp.where(first_step, h1i, h2i)
                return pltpu.roll(h2r, BATCH, 0), pltpu.roll(h2i, BATCH, 0)

            hr, hi = lax.fori_loop(0, TM_S5 // SUBLANES, body, (hre_ref[gb, :, re_c], him_ref[gb, :, re_c]))
            hre_ref[gb, :, re_c] = hr
            him_ref[gb, :, re_c] = hi
        y = jnp.dot(hs_s[...].astype(BF16), ccat_ref[gb], preferred_element_type=F32)
        y_s[:, cols] = y + dsk_ref[:, cols] * u[:, cols]
    _s5_glu(y_s, wglu_ref, bglu_ref, ya_ref)


def _s5_sample_kernel(u_ref, h0r_ref, h0i_ref, bcat_ref, ccat_ref, abr_ref, abi_ref, dsk_ref, wglu_ref,
                      bglu_ref, ya_ref, hre_ref, him_ref, bu_s, hs_s, y_s):
    u = u_ref[...]
    ub = u.astype(BF16)
    for gb in range(N_GB):
        cols = slice(gb * GB_CH, (gb + 1) * GB_CH)
        bu_s[...] = jnp.dot(ub[:, cols], bcat_ref[gb], preferred_element_type=F32)
        for lh in range(GB_ST // SCAN_LANES):
            re_c = slice(lh * SCAN_LANES, (lh + 1) * SCAN_LANES)
            im_c = slice(GB_ST + lh * SCAN_LANES, GB_ST + (lh + 1) * SCAN_LANES)
            st_c = slice(gb * GB_ST + lh * SCAN_LANES, gb * GB_ST + (lh + 1) * SCAN_LANES)
            ar = jnp.broadcast_to(abr_ref[gb, :, re_c], (SUBLANES, SCAN_LANES))
            ai = jnp.broadcast_to(abi_ref[gb, :, re_c], (SUBLANES, SCAN_LANES))

            def body(q, carry):
                r0 = pl.multiple_of(q * SUBLANES, SUBLANES)
                hr = h0r_ref[pl.ds(r0, SUBLANES), st_c]
                hi = h0i_ref[pl.ds(r0, SUBLANES), st_c]
                for t in range(DEC_SEQ):
                    rows = pl.ds(t * DEC_BATCH + r0, SUBLANES)
                    hr, hi = _cmul_add(ar, ai, hr, hi, bu_s[rows, re_c], bu_s[rows, im_c])
                    hs_s[rows, re_c] = hr
                    hs_s[rows, im_c] = hi
                hre_ref[pl.ds(r0, SUBLANES), st_c] = hr
                him_ref[pl.ds(r0, SUBLANES), st_c] = hi
                return carry

            lax.fori_loop(0, DEC_BATCH // SUBLANES, body, 0)
        y = jnp.dot(hs_s[...].astype(BF16), ccat_ref[gb], preferred_element_type=F32)
        y_s[:, cols] = y + dsk_ref[:, cols] * u[:, cols]
    _s5_glu(y_s, wglu_ref, bglu_ref, ya_ref)


def _s5_common_specs():
    return [
        _const_spec((N_GB, GB_CH, 2 * GB_ST)),
        _const_spec((N_GB, 2 * GB_ST, GB_CH)),
        _const_spec((N_GB, 1, GB_ST)), _const_spec((N_GB, 1, GB_ST)),
        _const_spec((1, D_SSM)),
        _const_spec((D_SSM, D_SSM)),
        _const_spec((1, D_SSM)),
    ]


def _s5_prompt(u_a, bcat, ccat, abr, abi, d_skip, w_glu_bf, b_glu):
    st = jax.ShapeDtypeStruct((N_GB, SUBLANES, GB_ST), F32)
    st_spec = pl.BlockSpec((N_GB, SUBLANES, GB_ST), lambda i: (0, 0, 0))
    return pl.pallas_call(
        _s5_prompt_kernel,
        grid=(N_P // TM_S5,),
        in_specs=[pl.BlockSpec((TM_S5, D_SSM), lambda i: (i, 0))] + _s5_common_specs(),
        out_specs=[pl.BlockSpec((TM_S5, D_SSM), lambda i: (i, 0)), st_spec, st_spec],
        out_shape=[jax.ShapeDtypeStruct((N_P, D_SSM), F32), st, st],
        scratch_shapes=[pltpu.VMEM((TM_S5, 2 * GB_ST), F32), pltpu.VMEM((TM_S5, 2 * GB_ST), F32),
                        pltpu.VMEM((TM_S5, D_SSM), F32)],
        compiler_params=_cparams("arbitrary"),
        name="s5_prompt",
    )(u_a, bcat, ccat, abr, abi, d_skip.reshape(1, D_SSM), w_glu_bf, b_glu.reshape(1, D_SSM))


def _s5_sample(u_a, h0_re, h0_im, bcat, ccat, abr, abi, d_skip, w_glu_bf, b_glu):
    st = jax.ShapeDtypeStruct((DEC_BATCH, N_SSM_GROUPS * SSM_STATE), F32)
    st_spec = pl.BlockSpec((DEC_BATCH, N_SSM_GROUPS * SSM_STATE), lambda i: (0, 0))
    return pl.pallas_call(
        _s5_sample_kernel,
        grid=(1,),
        in_specs=[pl.BlockSpec((N_S, D_SSM), lambda i: (N_P // N_S, 0)), st_spec, st_spec] + _s5_common_specs(),
        out_specs=[pl.BlockSpec((N_S, D_SSM), lambda i: (0, 0)), st_spec, st_spec],
        out_shape=[jax.ShapeDtypeStruct((N_S, D_SSM), F32), st, st],
        scratch_shapes=[pltpu.VMEM((N_S, 2 * GB_ST), F32), pltpu.VMEM((N_S, 2 * GB_ST), F32),
                        pltpu.VMEM((N_S, D_SSM), F32)],
        compiler_params=_cparams("arbitrary"),
        name="s5_sample",
    )(u_a, h0_re, h0_im, bcat, ccat, abr, abi, d_skip.reshape(1, D_SSM), w_glu_bf, b_glu.reshape(1, D_SSM))


def _gmlp_kernel(gub_ref, vn_ref, wk_ref, bsr_ref, ws4_ref, bs4_ref, yb_ref):
    i = pl.program_id(0)

    @pl.when(i < N_P // TM_MIX)
    def _prompt_chunk():
        vb = vn_ref[...].astype(BF16)
        row = lax.broadcasted_iota(I32, (TM_MIX, TM_MIX), 0)
        col = lax.broadcasted_iota(I32, (TM_MIX, TM_MIX), 1)
        for h in range(N_GMLP_HEADS):
            hc = slice(h * GMLP_HEAD, (h + 1) * GMLP_HEAD)
            w = jnp.where(row >= col, wk_ref[h], jnp.zeros((), BF16))
            mix = jnp.dot(w, vb[:, hc], preferred_element_type=F32) + bsr_ref[:, h:h + 1]
            yb_ref[:, hc] = gub_ref[:, hc] * mix

    @pl.when(i == N_P // TM_MIX)
    def _sample_chunk():
        for t in range(DEC_SEQ):
            acc = jnp.broadcast_to(bs4_ref[t:t + 1, :], (DEC_BATCH, D_GMLP))
            for s in range(t + 1):
                w_ts = ws4_ref[t * DEC_SEQ + s:t * DEC_SEQ + s + 1, :]
                acc = acc + w_ts * vn_ref[s * DEC_BATCH:(s + 1) * DEC_BATCH, :]
            rows = slice(t * DEC_BATCH, (t + 1) * DEC_BATCH)
            yb_ref[rows, :] = gub_ref[rows, :] * acc


def _gmlp(gu_b, vn, wk, bsr, ws4, bs4):
    tok = pl.BlockSpec((TM_MIX, D_GMLP), lambda i: (i, 0))
    return pl.pallas_call(
        _gmlp_kernel,
        grid=(N_TOK // TM_MIX,),
        in_specs=[
            tok, tok,
            _const_spec((N_GMLP_HEADS, TM_MIX, TM_MIX)),
            _const_spec((TM_MIX, N_GMLP_HEADS)),
            _const_spec((DEC_SEQ * DEC_SEQ, D_GMLP)),
            _const_spec((DEC_SEQ, D_GMLP)),
        ],
        out_specs=tok,
        out_shape=jax.ShapeDtypeStruct((N_TOK, D_GMLP), F32),
        compiler_params=_cparams("arbitrary"),
        name="gmlp_mix",
    )(gu_b, vn, wk, bsr, ws4, bs4)


def _mix_out_kernel(x_ref, ya_ref, yb_ref, goa_ref, gob_ref, wout_ref, gm_ref, shf_ref, scf_ref, gffn_ref,
                    wr_ref, br_ref, xnew_ref, h2_ref, idx_ref, gate_ref):
    na = (_rms(ya_ref[...]) * goa_ref[...]).astype(BF16)
    nb = (_rms(yb_ref[...]) * gob_ref[...]).astype(BF16)
    o = (jnp.dot(na, wout_ref[:D_SSM, :], preferred_element_type=F32)
         + jnp.dot(nb, wout_ref[D_SSM:, :], preferred_element_type=F32))
    x_new = x_ref[...] + _gate(o, gm_ref)
    xnew_ref[...] = x_new

    h2 = _modulate(_rms(x_new) * gffn_ref[...], scf_ref, shf_ref)
    for lt in range(ROW_TILES):
        h2_ref[pl.ds(lt, TM, stride=ROW_TILES), :] = h2[:, lt * LANES:(lt + 1) * LANES]

    logits = jnp.dot(h2, wr_ref[...], preferred_element_type=F32, precision=lax.Precision.HIGHEST) + br_ref[...]
    lane = lax.broadcasted_iota(I32, (TM, LANES), 1)
    work = logits
    idx_out = jnp.zeros((TM, LANES), I32)
    exp_out = jnp.zeros((TM, LANES), F32)
    denom = jnp.zeros((TM, 1), F32)
    top = None
    for k in range(TOP_K):
        m = jnp.max(work, axis=-1, keepdims=True)
        sel = jnp.min(jnp.where(work == m, lane, LANES), axis=-1, keepdims=True)
        top = m if k == 0 else top
        e = jnp.exp(m - top)
        denom = denom + e
        idx_out = jnp.where(lane == k, sel, idx_out)
        exp_out = jnp.where(lane == k, e, exp_out)
        work = jnp.where(lane == sel, -jnp.inf, work)
    idx_ref[...] = idx_out
    gate_ref[...] = exp_out / denom


def _mix_out(x, y_a, y_b, g_out_a, g_out_b, w_out_bf, mod, g_ffn, wr_pad, br_pad):
    p_tiles = N_P // TM
    tok = lambda w: pl.BlockSpec((TM, w), lambda i: (i, 0))
    mod_spec = lambda m: pl.BlockSpec((1, MOD_ROWS, D_MODEL), lambda i: (i // p_tiles, 0, m))
    return pl.pallas_call(
        _mix_out_kernel,
        grid=(N_TOK // TM,),
        in_specs=[
            tok(D_MODEL), tok(D_SSM), tok(D_GMLP),
            _const_spec((1, D_SSM)), _const_spec((1, D_GMLP)),
            _const_spec((D_MODEL, D_MODEL)),
            mod_spec(2), mod_spec(3), mod_spec(4),
            _const_spec((1, D_MODEL)),
            _const_spec((D_MODEL, LANES)), _const_spec((1, LANES)),
        ],
        out_specs=[
            tok(D_MODEL),
            pl.BlockSpec((TM * ROW_TILES, LANES), lambda i: (i, 0)),
            tok(LANES), tok(LANES),
        ],
        out_shape=[
            jax.ShapeDtypeStruct((N_TOK, D_MODEL), F32),
            jax.ShapeDtypeStruct((N_TOK * ROW_TILES, LANES), F32),
            jax.ShapeDtypeStruct((N_TOK, LANES), I32),
            jax.ShapeDtypeStruct((N_TOK, LANES), F32),
        ],
        compiler_params=_cparams("arbitrary"),
        name="mix_out_router",
    )(x, y_a, y_b, g_out_a.reshape(1, D_SSM), g_out_b.reshape(1, D_GMLP),
      w_out_bf, mod, mod, mod, g_ffn.reshape(1, D_MODEL), wr_pad, br_pad)


def _route(idx):
    flat_e = idx.reshape(-1)
    onehot = (flat_e[:, None] == jnp.arange(N_EXPERTS, dtype=I32)[None, :]).astype(I32)
    csum = jnp.cumsum(onehot, axis=0)
    rank = jnp.take_along_axis(csum, flat_e[:, None], axis=1)[:, 0] - 1
    counts = csum[-1]
    ntiles = (counts + ETM - 1) // ETM
    tile_end = jnp.cumsum(ntiles)
    tile0 = tile_end - ntiles
    dest = tile0[flat_e] * ETM + rank
    n_used = tile_end[-1]
    t = jnp.arange(N_ETILES, dtype=I32)
    tb = jnp.minimum(t, n_used - 1)
    te = jnp.minimum(jnp.sum((tile_end[None, :] <= tb[:, None]).astype(I32), axis=1), N_EXPERTS - 1)
    nv = jnp.where(t < n_used, jnp.clip(counts[te] - (t - tile0[te]) * ETM, 0, ETM), 0).astype(I32)
    full = counts // ESB * ESB
    zrow = jnp.where(counts % ESB != 0, tile0 * ETM + full, -1).astype(I32)
    return dest.astype(I32), te, nv, tb, zrow


def _row_start(row):
    return row * ROW_TILES if isinstance(row, int) else pl.multiple_of(row * ROW_TILES, ROW_TILES)


def _row_copy(src_hbm, src_row, dst_hbm, dst_row, n_rows, sem):
    return pltpu.make_async_copy(
        src_hbm.at[pl.ds(_row_start(src_row), n_rows * ROW_TILES)],
        dst_hbm.at[pl.ds(_row_start(dst_row), n_rows * ROW_TILES)],
        sem)


def _gather_kernel(zrow_ref, dest_ref, h2_ref, zeros_hbm, xs_hbm, sem_z, sem):
    @pl.when(pl.program_id(0) == 0)
    def _zero_partial_subblocks():
        for e in range(N_EXPERTS):
            @pl.when(zrow_ref[e] >= 0)
            def _():
                _row_copy(zeros_hbm, 0, xs_hbm, zrow_ref[e], ESB, sem_z).start()
        for e in range(N_EXPERTS):
            @pl.when(zrow_ref[e] >= 0)
            def _():
                _row_copy(zeros_hbm, 0, xs_hbm, zrow_ref[e], ESB, sem_z).wait()

    def body(a, carry):
        _row_copy(h2_ref, a // TOP_K, xs_hbm, dest_ref[0, 0, a], 1, sem).start()
        return carry

    lax.fori_loop(0, N_COMB, body, 0, unroll=8)
    for _ in range(TOP_K):
        _row_copy(h2_ref, 0, xs_hbm, 0, TM_COMB, sem).wait()


def _gather_rows(dest, zrow, h2_rows):
    zeros = jnp.zeros((ESB * ROW_TILES, LANES), F32)
    n_blk = N_TOK // TM_COMB
    return pl.pallas_call(
        _gather_kernel,
        grid_spec=pltpu.PrefetchScalarGridSpec(
            num_scalar_prefetch=1,
            grid=(n_blk,),
            in_specs=[
                pl.BlockSpec((1, 1, N_COMB), lambda i, z: (i, 0, 0), memory_space=pltpu.SMEM),
                pl.BlockSpec((TM_COMB * ROW_TILES, LANES), lambda i, z: (i, 0)),
                pl.BlockSpec(memory_space=pl.ANY),
            ],
            out_specs=pl.BlockSpec(memory_space=pl.ANY),
            scratch_shapes=[pltpu.SemaphoreType.DMA(()), pltpu.SemaphoreType.DMA(())],
        ),
        out_shape=jax.ShapeDtypeStruct((E_ROWS * ROW_TILES, LANES), F32),
        compiler_params=_cparams("arbitrary"),
        name="moe_gather",
    )(zrow, dest.reshape(n_blk, 1, N_COMB), h2_rows, zeros)


def _expert1_kernel(te_ref, nv_ref, tb_ref, x_ref, wg_ref, wl_ref, bg_ref, bl_ref, h_ref, xb_s):
    t = pl.program_id(0)
    j = pl.program_id(1)
    n = nv_ref[t]

    @pl.when(jnp.logical_and(j == 0, n > 0))
    def _to_matmul_layout():
        for sb in range(N_ESB):
            @pl.when(sb * ESB < n)
            def _():
                for lt in range(ROW_TILES):
                    v = x_ref[pl.ds(sb * ESB * ROW_TILES + lt, ESB, stride=ROW_TILES), :]
                    xb_s[sb * ESB:(sb + 1) * ESB, lt * LANES:(lt + 1) * LANES] = v.astype(BF16)

    @pl.when(n > 0)
    def _compute():
        wg = wg_ref[...].astype(BF16)
        wl = wl_ref[...].astype(BF16)

        def sub_block(sb):
            xs = xb_s[sb * ESB:(sb + 1) * ESB, :]
            glu = jnp.dot(xs, wg, preferred_element_type=F32) + bg_ref[...]
            lin = jnp.dot(xs, wl, preferred_element_type=F32) + bl_ref[...]
            glu = jnp.minimum(glu, SWIGLU_LIMIT)
            lin = jnp.clip(lin, -SWIGLU_LIMIT, SWIGLU_LIMIT)
            h_ref[sb * ESB:(sb + 1) * ESB, :] = (glu * jax.nn.sigmoid(SWIGLU_ALPHA * glu) * (lin + 1.0)).astype(BF16)

        sub_block(0)
        for sb in range(1, N_ESB):
            pl.when(sb * ESB < n)(functools.partial(sub_block, sb))


def _expert2_kernel(te_ref, nv_ref, tb_ref, h_ref, w_ref, b_ref, y_ref):
    t = pl.program_id(0)
    j = pl.program_id(1)
    n = nv_ref[t]

    @pl.when(n > 0)
    def _compute():
        w = w_ref[...].astype(BF16)

        def sub_block(sb):
            acc = jnp.dot(h_ref[sb * ESB:(sb + 1) * ESB, :], w, preferred_element_type=F32) + b_ref[...]
            for q in range(TN2 // LANES):
                start = sb * ESB * ROW_TILES + j * (TN2 // LANES) + q
                y_ref[pl.ds(start, ESB, stride=ROW_TILES), :] = acc[:, q * LANES:(q + 1) * LANES]

        sub_block(0)
        for sb in range(1, N_ESB):
            pl.when(sb * ESB < n)(functools.partial(sub_block, sb))


def _step(nv, t, j, last):
    return jnp.where(nv[t] > 0, j, last)


def _experts(layer, te, nv, tb, xs_rows, w1, b1, w2, b2):
    def w_spec(rows, cols, last, offset):
        return pl.BlockSpec((None, None, rows, cols),
                            lambda t, j, te, nv, tb: (layer, te[t], 0, offset + _step(nv, t, j, last)))

    b1r = b1.reshape(DEPTH, N_EXPERTS, 1, 2 * D_FF)
    h = pl.pallas_call(
        _expert1_kernel,
        grid_spec=pltpu.PrefetchScalarGridSpec(
            num_scalar_prefetch=3,
            grid=(N_ETILES, NJ1),
            in_specs=[
                pl.BlockSpec((ETM * ROW_TILES, LANES), lambda t, j, te, nv, tb: (tb[t], 0)),
                w_spec(D_MODEL, TF, NJ1 - 1, 0), w_spec(D_MODEL, TF, NJ1 - 1, NJ1),
                w_spec(1, TF, NJ1 - 1, 0), w_spec(1, TF, NJ1 - 1, NJ1),
            ],
            out_specs=pl.BlockSpec((ETM, TF), lambda t, j, te, nv, tb: (tb[t], _step(nv, t, j, NJ1 - 1))),
            scratch_shapes=[pltpu.VMEM((ETM, D_MODEL), BF16)],
        ),
        out_shape=jax.ShapeDtypeStruct((E_ROWS, D_FF), BF16),
        compiler_params=_cparams("arbitrary", "arbitrary"),
        name="moe_expert_in",
    )(te, nv, tb, xs_rows, w1, w1, b1r, b1r)
    return pl.pallas_call(
        _expert2_kernel,
        grid_spec=pltpu.PrefetchScalarGridSpec(
            num_scalar_prefetch=3,
            grid=(N_ETILES, NJ2),
            in_specs=[
                pl.BlockSpec((ETM, D_FF), lambda t, j, te, nv, tb: (tb[t], 0)),
                w_spec(D_FF, TN2, NJ2 - 1, 0), w_spec(1, TN2, NJ2 - 1, 0),
            ],
            out_specs=pl.BlockSpec((ETM * ROW_TILES, LANES), lambda t, j, te, nv, tb: (tb[t], 0)),
        ),
        out_shape=jax.ShapeDtypeStruct((E_ROWS * ROW_TILES, LANES), F32),
        compiler_params=_cparams("arbitrary", "arbitrary"),
        name="moe_expert_out",
    )(te, nv, tb, h, w2, b2.reshape(DEPTH, N_EXPERTS, 1, D_MODEL))


def _combine_kernel(final, dcur_ref, dnext_ref, y_hbm, x_ref, gate_ref, gf_ref, gfin_ref, o_ref, buf, sem):
    i = pl.program_id(0)
    n = pl.num_programs(0)
    slot = i % 2

    def issue(d_ref, s):
        def body(a, carry):
            tl = a // TOP_K
            k = a % TOP_K
            src = d_ref[0, 0, a]
            pltpu.make_async_copy(
                y_hbm.at[pl.ds(pl.multiple_of(src * ROW_TILES, ROW_TILES), ROW_TILES)],
                buf.at[s, pl.ds(pl.multiple_of((k * TM_COMB + tl) * ROW_PITCH, SUBLANES), ROW_TILES)],
                sem.at[s]).start()
            return carry
        lax.fori_loop(0, N_COMB, body, 0, unroll=8)

    @pl.when(i == 0)
    def _():
        issue(dcur_ref, 0)

    @pl.when(i + 1 < n)
    def _():
        issue(dnext_ref, 1 - slot)

    landed = buf.at[slot, pl.ds(0, N_COMB * ROW_TILES)]
    pltpu.make_async_copy(landed, landed, sem.at[slot]).wait()

    gates = gate_ref[...]
    x = x_ref[...]
    gf = gf_ref[0]
    pieces = []
    for lt in range(ROW_TILES):
        acc = jnp.zeros((TM_COMB, LANES), F32)
        for k in range(TOP_K):
            v = buf[slot, pl.ds(k * TM_COMB * ROW_PITCH + lt, TM_COMB, stride=ROW_PITCH), :]
            acc = acc + gates[:, k:k + 1] * v
        lc = slice(lt * LANES, (lt + 1) * LANES)
        pieces.append(x[:, lc] + gf[:, lc] * acc)
    out = jnp.concatenate(pieces, axis=-1)
    if final:
        out = _rms(out) * gfin_ref[...]
    o_ref[...] = out


def _combine(dest, y_rows, x_new, gates, mod, g_final, final):
    n_blk = N_TOK // TM_COMB
    p_tiles = N_P // TM_COMB
    d3 = dest.reshape(n_blk, 1, N_COMB)
    return pl.pallas_call(
        functools.partial(_combine_kernel, final),
        grid=(n_blk,),
        in_specs=[
            pl.BlockSpec((1, 1, N_COMB), lambda i: (i, 0, 0), memory_space=pltpu.SMEM),
            pl.BlockSpec((1, 1, N_COMB), lambda i: (jnp.minimum(i + 1, n_blk - 1), 0, 0), memory_space=pltpu.SMEM),
            pl.BlockSpec(memory_space=pl.ANY),
            pl.BlockSpec((TM_COMB, D_MODEL), lambda i: (i, 0)),
            pl.BlockSpec((TM_COMB, LANES), lambda i: (i, 0)),
            pl.BlockSpec((1, MOD_ROWS, D_MODEL), lambda i: (i // p_tiles, 0, 5)),
            _const_spec((1, D_MODEL)),
        ],
        out_specs=pl.BlockSpec((TM_COMB, D_MODEL), lambda i: (i, 0)),
        out_shape=jax.ShapeDtypeStruct((N_TOK, D_MODEL), F32),
        scratch_shapes=[pltpu.VMEM((2, N_COMB * ROW_PITCH, LANES), F32), pltpu.SemaphoreType.DMA((2,))],
        compiler_params=_cparams("arbitrary"),
        name="moe_combine",
    )(d3, d3, y_rows, x_new, gates, mod, g_final.reshape(1, D_MODEL))


def kernel(x_prompt, x_sample, c_prompt, c_sample, state_ssm_re, state_ssm_im, g_mix, w_ada, b_ada, w_in, lam_re, lam_im, log_dt, b_re, b_im, c_re, c_im, d_skip, w_glu, b_glu, ln_v_g, ln_v_b, w_s, b_s, g_out_a, g_out_b, w_out, g_ffn, w_router, b_router, w1, b1, w2, b2, g_final):
    x = jnp.concatenate([
        x_prompt.transpose(1, 0, 2).reshape(N_P, D_MODEL),
        x_sample.transpose(1, 0, 2).reshape(N_S, D_MODEL)], axis=0)

    c_all = jnp.concatenate([c_prompt, jnp.zeros((8 - BATCH, D_MODEL), F32), c_sample], axis=0)
    mod_all = _ada_mod(c_all, w_ada, b_ada)
    mod_tok = jnp.stack([jnp.tile(mod_all[:, :BATCH], (1, MOD_ROWS // BATCH, 1)), mod_all[:, 8:]], axis=1)

    ab_re, ab_im, bb_re, bb_im = _discretise(lam_re, lam_im, log_dt, b_re, b_im)
    bcat, ccat, abr, abi = _block_diag_params(ab_re, ab_im, bb_re, bb_im, c_re, c_im)

    eye_b = jnp.eye(BATCH, dtype=F32)
    wk = (w_s[:, :, :, None, :, None] * eye_b[None, None, None, :, None, :]).reshape(
        DEPTH, N_GMLP_HEADS, TM_MIX, TM_MIX).astype(BF16)
    bsr = jnp.repeat(b_s.transpose(0, 2, 1), BATCH, axis=1)
    ws4 = jnp.repeat(w_s[:, :, :DEC_SEQ, :DEC_SEQ].transpose(0, 2, 3, 1).reshape(
        DEPTH, DEC_SEQ * DEC_SEQ, N_GMLP_HEADS), GMLP_HEAD, axis=2)
    bs4 = jnp.repeat(b_s[:, :, :DEC_SEQ].transpose(0, 2, 1), GMLP_HEAD, axis=2)
    wr_pad = jnp.pad(w_router, ((0, 0), (0, 0), (0, LANES - N_EXPERTS)))
    br_pad = jnp.pad(b_router, ((0, 0), (0, LANES - N_EXPERTS)), constant_values=-1e30).reshape(DEPTH, 1, LANES)

    h0_re = state_ssm_re.reshape(DEPTH, DEC_BATCH, N_SSM_GROUPS * SSM_STATE)
    h0_im = state_ssm_im.reshape(DEPTH, DEC_BATCH, N_SSM_GROUPS * SSM_STATE)

    st_p_re, st_p_im, st_s_re, st_s_im, v_rows = [], [], [], [], []
    for l in range(DEPTH):
        mod = mod_tok[l]
        u_a, gu_b, vn = _mix_in(x, mod, g_mix[l], w_in[l].astype(BF16), ln_v_g[l], ln_v_b[l])
        wglu_bf = w_glu[l].astype(BF16)
        ya_p, hp_re, hp_im = _s5_prompt(u_a, bcat[l], ccat[l], abr[l], abi[l], d_skip[l], wglu_bf, b_glu[l])
        ya_s, hs_re, hs_im = _s5_sample(u_a, h0_re[l], h0_im[l], bcat[l], ccat[l], abr[l], abi[l], d_skip[l],
                                        wglu_bf, b_glu[l])
        y_a = jnp.concatenate([ya_p, ya_s], axis=0)
        y_b = _gmlp(gu_b, vn, wk[l], bsr[l], ws4[l], bs4[l])
        x_new, h2_rows, idx, gates = _mix_out(
            x, y_a, y_b, g_out_a[l], g_out_b[l], w_out[l].astype(BF16), mod, g_ffn[l], wr_pad[l], br_pad[l])
        dest, te, nv, tb, zrow = _route(idx[:, :TOP_K])
        xs_rows = _gather_rows(dest, zrow, h2_rows)
        y_rows = _experts(l, te, nv, tb, xs_rows, w1, b1, w2, b2)
        x = _combine(dest, y_rows, x_new, gates, mod, g_final, l == DEPTH - 1)

        unblock = lambda h: h[:, :BATCH].transpose(1, 0, 2).reshape(BATCH, N_SSM_GROUPS, SSM_STATE)
        st_p_re.append(unblock(hp_re))
        st_p_im.append(unblock(hp_im))
        st_s_re.append(hs_re.reshape(DEC_BATCH, N_SSM_GROUPS, SSM_STATE))
        st_s_im.append(hs_im.reshape(DEC_BATCH, N_SSM_GROUPS, SSM_STATE))
        v_rows.append(vn[N_P:].reshape(DEC_SEQ, DEC_BATCH, D_GMLP).transpose(1, 0, 2))

    y_prompt = x[:N_P].reshape(SEQ, BATCH, D_MODEL).transpose(1, 0, 2)
    y_sample = x[N_P:].reshape(DEC_SEQ, DEC_BATCH, D_MODEL).transpose(1, 0, 2)
    return (y_prompt, y_sample, jnp.stack(st_p_re), jnp.stack(st_p_im),
            jnp.stack(st_s_re), jnp.stack(st_s_im), jnp.stack(v_rows))
```

```python
import functools

import jax
import jax.numpy as jnp
from jax import lax
from jax.experimental import pallas as pl
from jax.experimental.pallas import tpu as pltpu

F32 = jnp.float32
BF16 = jnp.bfloat16
I32 = jnp.int32

D_MODEL = 2048
BATCH = 4
SEQ = 2048
DEPTH = 4
DEC_BATCH = 128
DEC_SEQ = 4
D_SSM = 1024
SSM_GROUP = 16
N_SSM_GROUPS = 64
SSM_STATE = 64
D_GMLP = 1024
GMLP_HEAD = 128
N_GMLP_HEADS = 8
CHUNK = 128
D_IN = D_SSM + 2 * D_GMLP
N_EXPERTS = 32
TOP_K = 4
D_FF = D_MODEL
SWIGLU_ALPHA = 1.702
SWIGLU_LIMIT = 7.0
N_MOD = 6
EPS = 1e-5

LANES = 128
SUBLANES = 8
N_P = BATCH * SEQ
N_S = DEC_BATCH * DEC_SEQ
N_TOK = N_P + N_S
N_ASSIGN = N_TOK * TOP_K
C_ROWS = 8 + DEC_BATCH
MOD_ROWS = 128

TM = 256
TM_S5 = 512
TM_MIX = CHUNK * BATCH
TM_COMB = 128
ROW_TILES = D_MODEL // LANES
ROW_PITCH = ROW_TILES + SUBLANES

GROUP_BLOCK = 16
N_GB = N_SSM_GROUPS // GROUP_BLOCK
GB_CH = GROUP_BLOCK * SSM_GROUP
GB_ST = GROUP_BLOCK * SSM_STATE
SCAN_LANES = 512

ESB = 256
N_ESB = 5
ETM = N_ESB * ESB
N_ETILES = N_ASSIGN // ETM + N_EXPERTS
E_ROWS = N_ETILES * ETM
TF = 512
NJ1 = D_FF // TF
TN2 = 512
NJ2 = D_MODEL // TN2
N_COMB = TM_COMB * TOP_K

VMEM_LIMIT = 56 * 1024 * 1024


def _cparams(*sem):
    return pltpu.CompilerParams(dimension_semantics=sem, vmem_limit_bytes=VMEM_LIMIT)


def _const_spec(shape):
    zeros = (0,) * len(shape)
    return pl.BlockSpec(shape, lambda *_: zeros, pipeline_mode=pl.Buffered(1))


def _rms(x):
    return x * lax.rsqrt(jnp.mean(x * x, axis=-1, keepdims=True) + EPS)


def _modulate(xn, sc_ref, sh_ref):
    rows, dim = xn.shape
    x3 = xn.reshape(rows // MOD_ROWS, MOD_ROWS, dim)
    return (x3 * (1.0 + sc_ref[0][None]) + sh_ref[0][None]).reshape(rows, dim)


def _gate(v, g_ref):
    rows, dim = v.shape
    return (v.reshape(rows // MOD_ROWS, MOD_ROWS, dim) * g_ref[0][None]).reshape(rows, dim)


def _ada_kernel(c_ref, w_ref, b_ref, o_ref):
    s = jax.nn.silu(c_ref[...]).astype(BF16)
    o_ref[0] = jnp.dot(s, w_ref[0].astype(BF16), preferred_element_type=F32) + b_ref[0]


def _ada_mod(c_all, w_ada, b_ada):
    tn = 1024
    nj = N_MOD * D_MODEL // tn
    return pl.pallas_call(
        _ada_kernel,
        grid=(DEPTH, nj),
        in_specs=[
            pl.BlockSpec((C_ROWS, D_MODEL), lambda l, j: (0, 0)),
            pl.BlockSpec((1, D_MODEL, tn), lambda l, j: (l, 0, j)),
            pl.BlockSpec((1, 1, tn), lambda l, j: (l, 0, j)),
        ],
        out_specs=pl.BlockSpec((1, C_ROWS, tn), lambda l, j: (l, 0, j)),
        out_shape=jax.ShapeDtypeStruct((DEPTH, C_ROWS, N_MOD * D_MODEL), F32),
        compiler_params=_cparams("arbitrary", "arbitrary"),
        name="ada_mod",
    )(c_all, w_ada, b_ada.reshape(DEPTH, 1, N_MOD * D_MODEL))


def _disc_kernel(lr_ref, li_ref, ldt_ref, br_ref, bi_ref, abr_ref, abi_ref, bbr_ref, bbi_ref):
    lr = lr_ref[...]
    li = li_ref[...]
    dt = jnp.exp(ldt_ref[...])
    mag = jnp.exp(lr * dt)
    ab_re = mag * jnp.cos(li * dt)
    ab_im = mag * jnp.sin(li * dt)
    den = lr * lr + li * li
    q_re = ((ab_re - 1.0) * lr + ab_im * li) / den
    q_im = (ab_im * lr - (ab_re - 1.0) * li) / den
    br = br_ref[...]
    bi = bi_ref[...]
    abr_ref[...] = ab_re
    abi_ref[...] = ab_im
    bbr_ref[...] = q_re * br - q_im * bi
    bbi_ref[...] = q_re * bi + q_im * br


def _discretise(lam_re, lam_im, log_dt, b_re, b_im):
    rows = DEPTH * N_SSM_GROUPS
    cols = SSM_STATE * SSM_GROUP
    rep = lambda a: jnp.repeat(a.reshape(rows, SSM_STATE), SSM_GROUP, axis=1)
    full = pl.BlockSpec((rows, cols), lambda: (0, 0))
    out = jax.ShapeDtypeStruct((rows, cols), F32)
    return pl.pallas_call(
        _disc_kernel,
        in_specs=[full, full, pl.BlockSpec((rows, 1), lambda: (0, 0)), full, full],
        out_specs=[full] * 4,
        out_shape=[out] * 4,
        name="s5_discretise",
    )(rep(lam_re), rep(lam_im), log_dt.reshape(rows, 1),
      b_re.reshape(rows, cols), b_im.reshape(rows, cols))


def _block_diag_params(ab_re, ab_im, bb_re, bb_im, c_re, c_im):
    eye = jnp.eye(GROUP_BLOCK, dtype=F32)
    shp = (DEPTH, N_GB, GROUP_BLOCK, SSM_STATE, SSM_GROUP)

    def in_mat(bb):
        m = bb.reshape(shp)[:, :, :, None, :, :] * eye[None, None, :, :, None, None]
        return m.transpose(0, 1, 2, 5, 3, 4).reshape(DEPTH, N_GB, GB_CH, GB_ST)

    def out_mat(cc):
        c5 = cc.reshape(DEPTH, N_GB, GROUP_BLOCK, SSM_GROUP, SSM_STATE)
        m = c5[:, :, :, None, :, :] * eye[None, None, :, :, None, None]
        return m.transpose(0, 1, 2, 5, 3, 4).reshape(DEPTH, N_GB, GB_ST, GB_CH)

    bcat = jnp.concatenate([in_mat(bb_re), in_mat(bb_im)], axis=-1).astype(BF16)
    ccat = jnp.concatenate([out_mat(c_re), -out_mat(c_im)], axis=-2).astype(BF16)
    take = lambda a: a.reshape(DEPTH, N_GB, GROUP_BLOCK, SSM_STATE, SSM_GROUP)[..., 0].reshape(DEPTH, N_GB, 1, GB_ST)
    return bcat, ccat, take(ab_re), take(ab_im)


def _mix_in_kernel(x_ref, sh_ref, sc_ref, g_ref, w_ref, lng_ref, lnb_ref, ua_ref, gub_ref, vn_ref):
    xn = _rms(x_ref[...]) * g_ref[...]
    h = _modulate(xn, sc_ref, sh_ref)
    proj = jnp.dot(h.astype(BF16), w_ref[...], preferred_element_type=F32)
    ua_ref[...] = proj[:, :D_SSM]
    gub_ref[...] = jax.nn.gelu(proj[:, D_SSM:D_SSM + D_GMLP])
    gv = jax.nn.gelu(proj[:, D_SSM + D_GMLP:])
    mu = jnp.mean(gv, axis=-1, keepdims=True)
    var = jnp.mean(jnp.square(gv - mu), axis=-1, keepdims=True)
    vn_ref[...] = (gv - mu) * lax.rsqrt(var + EPS) * lng_ref[...] + lnb_ref[...]


def _mix_in(x, mod, g_mix, w_in_bf, ln_g, ln_b):
    p_tiles = N_P // TM
    mod_spec = lambda m: pl.BlockSpec((1, MOD_ROWS, D_MODEL), lambda i: (i // p_tiles, 0, m))
    out = jax.ShapeDtypeStruct((N_TOK, D_SSM), F32)
    return pl.pallas_call(
        _mix_in_kernel,
        grid=(N_TOK // TM,),
        in_specs=[
            pl.BlockSpec((TM, D_MODEL), lambda i: (i, 0)),
            mod_spec(0), mod_spec(1),
            _const_spec((1, D_MODEL)),
            _const_spec((D_MODEL, D_IN)),
            _const_spec((1, D_GMLP)), _const_spec((1, D_GMLP)),
        ],
        out_specs=[pl.BlockSpec((TM, D_SSM), lambda i: (i, 0))] * 3,
        out_shape=[out] * 3,
        compiler_params=_cparams("arbitrary"),
        name="mix_in",
    )(x, mod, mod, g_mix.reshape(1, D_MODEL), w_in_bf, ln_g.reshape(1, D_GMLP), ln_b.reshape(1, D_GMLP))


def _cmul_add(ar, ai, hr, hi, vr, vi):
    return ar * hr - ai * hi + vr, ar * hi + ai * hr + vi


def _s5_glu(y_s, wglu_ref, bglu_ref, ya_ref):
    ya = jax.nn.gelu(y_s[...])
    z = jnp.dot(ya.astype(BF16), wglu_ref[...], preferred_element_type=F32) + bglu_ref[...]
    ya_ref[...] = ya * jax.nn.sigmoid(z)


def _s5_prompt_kernel(u_ref, bcat_ref, ccat_ref, abr_ref, abi_ref, dsk_ref, wglu_ref, bglu_ref,
                      ya_ref, hre_ref, him_ref, bu_s, hs_s, y_s):
    @pl.when(pl.program_id(0) == 0)
    def _():
        hre_ref[...] = jnp.zeros_like(hre_ref)
        him_ref[...] = jnp.zeros_like(him_ref)

    u = u_ref[...]
    ub = u.astype(BF16)
    first_step = lax.broadcasted_iota(I32, (SUBLANES, SCAN_LANES), 0) < BATCH
    for gb in range(N_GB):
        cols = slice(gb * GB_CH, (gb + 1) * GB_CH)
        bu_s[...] = jnp.dot(ub[:, cols], bcat_ref[gb], preferred_element_type=F32)
        for lh in range(GB_ST // SCAN_LANES):
            re_c = slice(lh * SCAN_LANES, (lh + 1) * SCAN_LANES)
            im_c = slice(GB_ST + lh * SCAN_LANES, GB_ST + (lh + 1) * SCAN_LANES)
            ar = jnp.broadcast_to(abr_ref[gb, :, re_c], (SUBLANES, SCAN_LANES))
            ai = jnp.broadcast_to(abi_ref[gb, :, re_c], (SUBLANES, SCAN_LANES))

            def body(k, carry):
                hr, hi = carry
                rows = pl.ds(pl.multiple_of(k * SUBLANES, SUBLANES), SUBLANES)
                vr = bu_s[rows, re_c]
                vi = bu_s[rows, im_c]
                h1r, h1i = _cmul_add(ar, ai, hr, hi, vr, vi)
                h2r, h2i = _cmul_add(ar, ai, pltpu.roll(h1r, BATCH, 0), pltpu.roll(h1i, BATCH, 0), vr, vi)
                hs_s[rows, re_c] = jnp.where(first_step, h1r, h2r)
                hs_s[rows, im_c] = jnp.where(first_step, h1i, h2i)
                return pltpu.roll(h2r, BATCH, 0), pltpu.roll(h2i, BATCH, 0)

            hr, hi = lax.fori_loop(0, TM_S5 // SUBLANES, body, (hre_ref[gb, :, re_c], him_ref[gb, :, re_c]))
            hre_ref[gb, :, re_c] = hr
            him_ref[gb, :, re_c] = hi
        y = jnp.dot(hs_s[...].astype(BF16), ccat_ref[gb], preferred_element_type=F32)
        y_s[:, cols] = y + dsk_ref[:, cols] * u[:, cols]
    _s5_glu(y_s, wglu_ref, bglu_ref, ya_ref)


def _s5_sample_kernel(u_ref, h0r_ref, h0i_ref, bcat_ref, ccat_ref, abr_ref, abi_ref, dsk_ref, wglu_ref,
                      bglu_ref, ya_ref, hre_ref, him_ref, bu_s, hs_s, y_s):
    u = u_ref[...]
    ub = u.astype(BF16)
    for gb in range(N_GB):
        cols = slice(gb * GB_CH, (gb + 1) * GB_CH)
        bu_s[...] = jnp.dot(ub[:, cols], bcat_ref[gb], preferred_element_type=F32)
        for lh in range(GB_ST // SCAN_LANES):
            re_c = slice(lh * SCAN_LANES, (lh + 1) * SCAN_LANES)
            im_c = slice(GB_ST + lh * SCAN_LANES, GB_ST + (lh + 1) * SCAN_LANES)
            st_c = slice(gb * GB_ST + lh * SCAN_LANES, gb * GB_ST + (lh + 1) * SCAN_LANES)
            ar = jnp.broadcast_to(abr_ref[gb, :, re_c], (SUBLANES, SCAN_LANES))
            ai = jnp.broadcast_to(abi_ref[gb, :, re_c], (SUBLANES, SCAN_LANES))

            def body(q, carry):
                r0 = pl.multiple_of(q * SUBLANES, SUBLANES)
                hr = h0r_ref[pl.ds(r0, SUBLANES), st_c]
                hi = h0i_ref[pl.ds(r0, SUBLANES), st_c]
                for t in range(DEC_SEQ):
                    rows = pl.ds(t * DEC_BATCH + r0, SUBLANES)
                    hr, hi = _cmul_add(ar, ai, hr, hi, bu_s[rows, re_c], bu_s[rows, im_c])
                    hs_s[rows, re_c] = hr
                    hs_s[rows, im_c] = hi
                hre_ref[pl.ds(r0, SUBLANES), st_c] = hr
                him_ref[pl.ds(r0, SUBLANES), st_c] = hi
                return carry

            lax.fori_loop(0, DEC_BATCH // SUBLANES, body, 0)
        y = jnp.dot(hs_s[...].astype(BF16), ccat_ref[gb], preferred_element_type=F32)
        y_s[:, cols] = y + dsk_ref[:, cols] * u[:, cols]
    _s5_glu(y_s, wglu_ref, bglu_ref, ya_ref)


def _s5_common_specs():
    return [
        _const_spec((N_GB, GB_CH, 2 * GB_ST)),
        _const_spec((N_GB, 2 * GB_ST, GB_CH)),
        _const_spec((N_GB, 1, GB_ST)), _const_spec((N_GB, 1, GB_ST)),
        _const_spec((1, D_SSM)),
        _const_spec((D_SSM, D_SSM)),
        _const_spec((1, D_SSM)),
    ]


def _s5_prompt(u_a, bcat, ccat, abr, abi, d_skip, w_glu_bf, b_glu):
    st = jax.ShapeDtypeStruct((N_GB, SUBLANES, GB_ST), F32)
    st_spec = pl.BlockSpec((N_GB, SUBLANES, GB_ST), lambda i: (0, 0, 0))
    return pl.pallas_call(
        _s5_prompt_kernel,
        grid=(N_P // TM_S5,),
        in_specs=[pl.BlockSpec((TM_S5, D_SSM), lambda i: (i, 0))] + _s5_common_specs(),
        out_specs=[pl.BlockSpec((TM_S5, D_SSM), lambda i: (i, 0)), st_spec, st_spec],
        out_shape=[jax.ShapeDtypeStruct((N_P, D_SSM), F32), st, st],
        scratch_shapes=[pltpu.VMEM((TM_S5, 2 * GB_ST), F32), pltpu.VMEM((TM_S5, 2 * GB_ST), F32),
                        pltpu.VMEM((TM_S5, D_SSM), F32)],
        compiler_params=_cparams("arbitrary"),
        name="s5_prompt",
    )(u_a, bcat, ccat, abr, abi, d_skip.reshape(1, D_SSM), w_glu_bf, b_glu.reshape(1, D_SSM))


def _s5_sample(u_a, h0_re, h0_im, bcat, ccat, abr, abi, d_skip, w_glu_bf, b_glu):
    st = jax.ShapeDtypeStruct((DEC_BATCH, N_SSM_GROUPS * SSM_STATE), F32)
    st_spec = pl.BlockSpec((DEC_BATCH, N_SSM_GROUPS * SSM_STATE), lambda i: (0, 0))
    return pl.pallas_call(
        _s5_sample_kernel,
        grid=(1,),
        in_specs=[pl.BlockSpec((N_S, D_SSM), lambda i: (N_P // N_S, 0)), st_spec, st_spec] + _s5_common_specs(),
        out_specs=[pl.BlockSpec((N_S, D_SSM), lambda i: (0, 0)), st_spec, st_spec],
        out_shape=[jax.ShapeDtypeStruct((N_S, D_SSM), F32), st, st],
        scratch_shapes=[pltpu.VMEM((N_S, 2 * GB_ST), F32), pltpu.VMEM((N_S, 2 * GB_ST), F32),
                        pltpu.VMEM((N_S, D_SSM), F32)],
        compiler_params=_cparams("arbitrary"),
        name="s5_sample",
    )(u_a, h0_re, h0_im, bcat, ccat, abr, abi, d_skip.reshape(1, D_SSM), w_glu_bf, b_glu.reshape(1, D_SSM))


def _gmlp_kernel(gub_ref, vn_ref, wk_ref, bsr_ref, ws4_ref, bs4_ref, yb_ref):
    i = pl.program_id(0)

    @pl.when(i < N_P // TM_MIX)
    def _prompt_chunk():
        vb = vn_ref[...].astype(BF16)
        row = lax.broadcasted_iota(I32, (TM_MIX, TM_MIX), 0)
        col = lax.broadcasted_iota(I32, (TM_MIX, TM_MIX), 1)
        for h in range(N_GMLP_HEADS):
            hc = slice(h * GMLP_HEAD, (h + 1) * GMLP_HEAD)
            w = jnp.where(row >= col, wk_ref[h], jnp.zeros((), BF16))
            mix = jnp.dot(w, vb[:, hc], preferred_element_type=F32) + bsr_ref[:, h:h + 1]
            yb_ref[:, hc] = gub_ref[:, hc] * mix

    @pl.when(i == N_P // TM_MIX)
    def _sample_chunk():
        for t in range(DEC_SEQ):
            acc = jnp.broadcast_to(bs4_ref[t:t + 1, :], (DEC_BATCH, D_GMLP))
            for s in range(t + 1):
                w_ts = ws4_ref[t * DEC_SEQ + s:t * DEC_SEQ + s + 1, :]
                acc = acc + w_ts * vn_ref[s * DEC_BATCH:(s + 1) * DEC_BATCH, :]
            rows = slice(t * DEC_BATCH, (t + 1) * DEC_BATCH)
            yb_ref[rows, :] = gub_ref[rows, :] * acc


def _gmlp(gu_b, vn, wk, bsr, ws4, bs4):
    tok = pl.BlockSpec((TM_MIX, D_GMLP), lambda i: (i, 0))
    return pl.pallas_call(
        _gmlp_kernel,
        grid=(N_TOK // TM_MIX,),
        in_specs=[
            tok, tok,
            _const_spec((N_GMLP_HEADS, TM_MIX, TM_MIX)),
            _const_spec((TM_MIX, N_GMLP_HEADS)),
            _const_spec((DEC_SEQ * DEC_SEQ, D_GMLP)),
            _const_spec((DEC_SEQ, D_GMLP)),
        ],
        out_specs=tok,
        out_shape=jax.ShapeDtypeStruct((N_TOK, D_GMLP), F32),
        compiler_params=_cparams("arbitrary"),
        name="gmlp_mix",
    )(gu_b, vn, wk, bsr, ws4, bs4)


def _mix_out_kernel(x_ref, ya_ref, yb_ref, goa_ref, gob_ref, wout_ref, gm_ref, shf_ref, scf_ref, gffn_ref,
                    wr_ref, br_ref, xnew_ref, h2_ref, idx_ref, gate_ref):
    na = (_rms(ya_ref[...]) * goa_ref[...]).astype(BF16)
    nb = (_rms(yb_ref[...]) * gob_ref[...]).astype(BF16)
    o = (jnp.dot(na, wout_ref[:D_SSM, :], preferred_element_type=F32)
         + jnp.dot(nb, wout_ref[D_SSM:, :], preferred_element_type=F32))
    x_new = x_ref[...] + _gate(o, gm_ref)
    xnew_ref[...] = x_new

    h2 = _modulate(_rms(x_new) * gffn_ref[...], scf_ref, shf_ref)
    for lt in range(ROW_TILES):
        h2_ref[pl.ds(lt, TM, stride=ROW_TILES), :] = h2[:, lt * LANES:(lt + 1) * LANES]

    logits = jnp.dot(h2, wr_ref[...], preferred_element_type=F32, precision=lax.Precision.HIGHEST) + br_ref[...]
    lane = lax.broadcasted_iota(I32, (TM, LANES), 1)
    work = logits
    idx_out = jnp.zeros((TM, LANES), I32)
    exp_out = jnp.zeros((TM, LANES), F32)
    denom = jnp.zeros((TM, 1), F32)
    top = None
    for k in range(TOP_K):
        m = jnp.max(work, axis=-1, keepdims=True)
        sel = jnp.min(jnp.where(work == m, lane, LANES), axis=-1, keepdims=True)
        top = m if k == 0 else top
        e = jnp.exp(m - top)
        denom = denom + e
        idx_out = jnp.where(lane == k, sel, idx_out)
        exp_out = jnp.where(lane == k, e, exp_out)
        work = jnp.where(lane == sel, -jnp.inf, work)
    idx_ref[...] = idx_out
    gate_ref[...] = exp_out / denom


def _mix_out(x, y_a, y_b, g_out_a, g_out_b, w_out_bf, mod, g_ffn, wr_pad, br_pad):
    p_tiles = N_P // TM
    tok = lambda w: pl.BlockSpec((TM, w), lambda i: (i, 0))
    mod_spec = lambda m: pl.BlockSpec((1, MOD_ROWS, D_MODEL), lambda i: (i // p_tiles, 0, m))
    return pl.pallas_call(
        _mix_out_kernel,
        grid=(N_TOK // TM,),
        in_specs=[
            tok(D_MODEL), tok(D_SSM), tok(D_GMLP),
            _const_spec((1, D_SSM)), _const_spec((1, D_GMLP)),
            _const_spec((D_MODEL, D_MODEL)),
            mod_spec(2), mod_spec(3), mod_spec(4),
            _const_spec((1, D_MODEL)),
            _const_spec((D_MODEL, LANES)), _const_spec((1, LANES)),
        ],
        out_specs=[
            tok(D_MODEL),
            pl.BlockSpec((TM * ROW_TILES, LANES), lambda i: (i, 0)),
            tok(LANES), tok(LANES),
        ],
        out_shape=[
            jax.ShapeDtypeStruct((N_TOK, D_MODEL), F32),
            jax.ShapeDtypeStruct((N_TOK * ROW_TILES, LANES), F32),
            jax.ShapeDtypeStruct((N_TOK, LANES), I32),
            jax.ShapeDtypeStruct((N_TOK, LANES), F32),
        ],
        compiler_params=_cparams("arbitrary"),
        name="mix_out_router",
    )(x, y_a, y_b, g_out_a.reshape(1, D_SSM), g_out_b.reshape(1, D_GMLP),
      w_out_bf, mod, mod, mod, g_ffn.reshape(1, D_MODEL), wr_pad, br_pad)


def _route(idx):
    flat_e = idx.reshape(-1)
    onehot = (flat_e[:, None] == jnp.arange(N_EXPERTS, dtype=I32)[None, :]).astype(I32)
    csum = jnp.cumsum(onehot, axis=0)
    rank = jnp.take_along_axis(csum, flat_e[:, None], axis=1)[:, 0] - 1
    counts = csum[-1]
    ntiles = (counts + ETM - 1) // ETM
    tile_end = jnp.cumsum(ntiles)
    tile0 = tile_end - ntiles
    dest = tile0[flat_e] * ETM + rank
    n_used = tile_end[-1]
    t = jnp.arange(N_ETILES, dtype=I32)
    tb = jnp.minimum(t, n_used - 1)
    te = jnp.minimum(jnp.sum((tile_end[None, :] <= tb[:, None]).astype(I32), axis=1), N_EXPERTS - 1)
    nv = jnp.where(t < n_used, jnp.clip(counts[te] - (t - tile0[te]) * ETM, 0, ETM), 0).astype(I32)
    full = counts // ESB * ESB
    zrow = jnp.where(counts % ESB != 0, tile0 * ETM + full, -1).astype(I32)
    return dest.astype(I32), te, nv, tb, zrow


def _row_start(row):
    return row * ROW_TILES if isinstance(row, int) else pl.multiple_of(row * ROW_TILES, ROW_TILES)


def _row_copy(src_hbm, src_row, dst_hbm, dst_row, n_rows, sem):
    return pltpu.make_async_copy(
        src_hbm.at[pl.ds(_row_start(src_row), n_rows * ROW_TILES)],
        dst_hbm.at[pl.ds(_row_start(dst_row), n_rows * ROW_TILES)],
        sem)


def _gather_kernel(zrow_ref, dest_ref, h2_ref, xs_hbm, zeros_v, sem_z, sem):
    @pl.when(pl.program_id(0) == 0)
    def _zero_partial_subblocks():
        zeros_v[...] = jnp.zeros_like(zeros_v)
        for e in range(N_EXPERTS):
            @pl.when(zrow_ref[e] >= 0)
            def _():
                _row_copy(zeros_v, 0, xs_hbm, zrow_ref[e], ESB, sem_z).start()
        for e in range(N_EXPERTS):
            @pl.when(zrow_ref[e] >= 0)
            def _():
                _row_copy(zeros_v, 0, xs_hbm, zrow_ref[e], ESB, sem_z).wait()

    def body(tl, carry):
        for k in range(TOP_K):
            _row_copy(h2_ref, tl, xs_hbm, dest_ref[0, 0, tl * TOP_K + k], 1, sem).start()
        return carry

    lax.fori_loop(0, TM_COMB, body, 0, unroll=2)
    for _ in range(TOP_K):
        _row_copy(h2_ref, 0, xs_hbm, 0, TM_COMB, sem).wait()


def _gather_rows(dest, zrow, h2_rows):
    n_blk = N_TOK // TM_COMB
    return pl.pallas_call(
        _gather_kernel,
        grid_spec=pltpu.PrefetchScalarGridSpec(
            num_scalar_prefetch=1,
            grid=(n_blk,),
            in_specs=[
                pl.BlockSpec((1, 1, N_COMB), lambda i, z: (i, 0, 0), memory_space=pltpu.SMEM),
                pl.BlockSpec((TM_COMB * ROW_TILES, LANES), lambda i, z: (i, 0)),
            ],
            out_specs=pl.BlockSpec(memory_space=pl.ANY),
            scratch_shapes=[pltpu.VMEM((ESB * ROW_TILES, LANES), F32),
                            pltpu.SemaphoreType.DMA(()), pltpu.SemaphoreType.DMA(())],
        ),
        out_shape=jax.ShapeDtypeStruct((E_ROWS * ROW_TILES, LANES), F32),
        compiler_params=_cparams("arbitrary"),
        name="moe_gather",
    )(zrow, dest.reshape(n_blk, 1, N_COMB), h2_rows)


def _expert1_kernel(te_ref, nv_ref, tb_ref, x_ref, wg_ref, wl_ref, bg_ref, bl_ref, h_ref, xb_s):
    t = pl.program_id(0)
    j = pl.program_id(1)
    n = nv_ref[t]

    @pl.when(jnp.logical_and(j == 0, n > 0))
    def _to_matmul_layout():
        for sb in range(N_ESB):
            @pl.when(sb * ESB < n)
            def _():
                for lt in range(ROW_TILES):
                    v = x_ref[pl.ds(sb * ESB * ROW_TILES + lt, ESB, stride=ROW_TILES), :]
                    xb_s[sb * ESB:(sb + 1) * ESB, lt * LANES:(lt + 1) * LANES] = v.astype(BF16)

    @pl.when(n > 0)
    def _compute():
        wg = wg_ref[...].astype(BF16)
        wl = wl_ref[...].astype(BF16)

        def sub_block(sb):
            xs = xb_s[sb * ESB:(sb + 1) * ESB, :]
            glu = jnp.dot(xs, wg, preferred_element_type=F32) + bg_ref[...]
            lin = jnp.dot(xs, wl, preferred_element_type=F32) + bl_ref[...]
            glu = jnp.minimum(glu, SWIGLU_LIMIT)
            lin = jnp.clip(lin, -SWIGLU_LIMIT, SWIGLU_LIMIT)
            h_ref[sb * ESB:(sb + 1) * ESB, :] = (glu * jax.nn.sigmoid(SWIGLU_ALPHA * glu) * (lin + 1.0)).astype(BF16)

        sub_block(0)
        for sb in range(1, N_ESB):
            pl.when(sb * ESB < n)(functools.partial(sub_block, sb))


def _expert2_kernel(te_ref, nv_ref, tb_ref, h_ref, w_ref, b_ref, y_ref):
    t = pl.program_id(0)
    j = pl.program_id(1)
    n = nv_ref[t]

    @pl.when(n > 0)
    def _compute():
        w = w_ref[...].astype(BF16)

        def sub_block(sb):
            acc = jnp.dot(h_ref[sb * ESB:(sb + 1) * ESB, :], w, preferred_element_type=F32) + b_ref[...]
            for q in range(TN2 // LANES):
                start = sb * ESB * ROW_TILES + j * (TN2 // LANES) + q
                y_ref[pl.ds(start, ESB, stride=ROW_TILES), :] = acc[:, q * LANES:(q + 1) * LANES]

        sub_block(0)
        for sb in range(1, N_ESB):
            pl.when(sb * ESB < n)(functools.partial(sub_block, sb))


def _step(nv, t, j, last):
    return jnp.where(nv[t] > 0, j, last)


def _experts(layer, te, nv, tb, xs_rows, w1, b1, w2, b2):
    def w_spec(rows, cols, last, offset):
        return pl.BlockSpec((None, None, rows, cols),
                            lambda t, j, te, nv, tb: (layer, te[t], 0, offset + _step(nv, t, j, last)))

    b1r = b1.reshape(DEPTH, N_EXPERTS, 1, 2 * D_FF)
    h = pl.pallas_call(
        _expert1_kernel,
        grid_spec=pltpu.PrefetchScalarGridSpec(
            num_scalar_prefetch=3,
            grid=(N_ETILES, NJ1),
            in_specs=[
                pl.BlockSpec((ETM * ROW_TILES, LANES), lambda t, j, te, nv, tb: (tb[t], 0)),
                w_spec(D_MODEL, TF, NJ1 - 1, 0), w_spec(D_MODEL, TF, NJ1 - 1, NJ1),
                w_spec(1, TF, NJ1 - 1, 0), w_spec(1, TF, NJ1 - 1, NJ1),
            ],
            out_specs=pl.BlockSpec((ETM, TF), lambda t, j, te, nv, tb: (tb[t], _step(nv, t, j, NJ1 - 1))),
            scratch_shapes=[pltpu.VMEM((ETM, D_MODEL), BF16)],
        ),
        out_shape=jax.ShapeDtypeStruct((E_ROWS, D_FF), BF16),
        compiler_params=_cparams("arbitrary", "arbitrary"),
        name="moe_expert_in",
    )(te, nv, tb, xs_rows, w1, w1, b1r, b1r)
    return pl.pallas_call(
        _expert2_kernel,
        grid_spec=pltpu.PrefetchScalarGridSpec(
            num_scalar_prefetch=3,
            grid=(N_ETILES, NJ2),
            in_specs=[
                pl.BlockSpec((ETM, D_FF), lambda t, j, te, nv, tb: (tb[t], 0)),
                w_spec(D_FF, TN2, NJ2 - 1, 0), w_spec(1, TN2, NJ2 - 1, 0),
            ],
            out_specs=pl.BlockSpec((ETM * ROW_TILES, LANES), lambda t, j, te, nv, tb: (tb[t], 0)),
        ),
        out_shape=jax.ShapeDtypeStruct((E_ROWS * ROW_TILES, LANES), F32),
        compiler_params=_cparams("arbitrary", "arbitrary"),
        name="moe_expert_out",
    )(te, nv, tb, h, w2, b2.reshape(DEPTH, N_EXPERTS, 1, D_MODEL))


def _combine_kernel(final, dcur_ref, dnext_ref, y_hbm, x_ref, gate_ref, gf_ref, gfin_ref, o_ref, buf, sem):
    i = pl.program_id(0)
    n = pl.num_programs(0)
    slot = i % 2

    def issue(d_ref, s):
        def body(tl, carry):
            for k in range(TOP_K):
                src = d_ref[0, 0, tl * TOP_K + k]
                pltpu.make_async_copy(
                    y_hbm.at[pl.ds(pl.multiple_of(src * ROW_TILES, ROW_TILES), ROW_TILES)],
                    buf.at[s, pl.ds(pl.multiple_of((k * TM_COMB + tl) * ROW_PITCH, SUBLANES), ROW_TILES)],
                    sem.at[s]).start()
            return carry
        lax.fori_loop(0, TM_COMB, body, 0, unroll=2)

    @pl.when(i == 0)
    def _():
        issue(dcur_ref, 0)

    @pl.when(i + 1 < n)
    def _():
        issue(dnext_ref, 1 - slot)

    landed = buf.at[slot, pl.ds(0, N_COMB * ROW_TILES)]
    pltpu.make_async_copy(landed, landed, sem.at[slot]).wait()

    gates = gate_ref[...]
    x = x_ref[...]
    gf = gf_ref[0]
    pieces = []
    for lt in range(ROW_TILES):
        acc = jnp.zeros((TM_COMB, LANES), F32)
        for k in range(TOP_K):
            v = buf[slot, pl.ds(k * TM_COMB * ROW_PITCH + lt, TM_COMB, stride=ROW_PITCH), :]
            acc = acc + gates[:, k:k + 1] * v
        lc = slice(lt * LANES, (lt + 1) * LANES)
        pieces.append(x[:, lc] + gf[:, lc] * acc)
    out = jnp.concatenate(pieces, axis=-1)
    if final:
        out = _rms(out) * gfin_ref[...]
    o_ref[...] = out


def _combine(dest, y_rows, x_new, gates, mod, g_final, final):
    n_blk = N_TOK // TM_COMB
    p_tiles = N_P // TM_COMB
    d3 = dest.reshape(n_blk, 1, N_COMB)
    return pl.pallas_call(
        functools.partial(_combine_kernel, final),
        grid=(n_blk,),
        in_specs=[
            pl.BlockSpec((1, 1, N_COMB), lambda i: (i, 0, 0), memory_space=pltpu.SMEM),
            pl.BlockSpec((1, 1, N_COMB), lambda i: (jnp.minimum(i + 1, n_blk - 1), 0, 0), memory_space=pltpu.SMEM),
            pl.BlockSpec(memory_space=pl.ANY),
            pl.BlockSpec((TM_COMB, D_MODEL), lambda i: (i, 0)),
            pl.BlockSpec((TM_COMB, LANES), lambda i: (i, 0)),
            pl.BlockSpec((1, MOD_ROWS, D_MODEL), lambda i: (i // p_tiles, 0, 5)),
            _const_spec((1, D_MODEL)),
        ],
        out_specs=pl.BlockSpec((TM_COMB, D_MODEL), lambda i: (i, 0)),
        out_shape=jax.ShapeDtypeStruct((N_TOK, D_MODEL), F32),
        scratch_shapes=[pltpu.VMEM((2, N_COMB * ROW_PITCH, LANES), F32), pltpu.SemaphoreType.DMA((2,))],
        compiler_params=_cparams("arbitrary"),
        name="moe_combine",
    )(d3, d3, y_rows, x_new, gates, mod, g_final.reshape(1, D_MODEL))


def kernel(x_prompt, x_sample, c_prompt, c_sample, state_ssm_re, state_ssm_im, g_mix, w_ada, b_ada, w_in, lam_re, lam_im, log_dt, b_re, b_im, c_re, c_im, d_skip, w_glu, b_glu, ln_v_g, ln_v_b, w_s, b_s, g_out_a, g_out_b, w_out, g_ffn, w_router, b_router, w1, b1, w2, b2, g_final):
    x = jnp.concatenate([
        x_prompt.transpose(1, 0, 2).reshape(N_P, D_MODEL),
        x_sample.transpose(1, 0, 2).reshape(N_S, D_MODEL)], axis=0)

    c_all = jnp.concatenate([c_prompt, jnp.zeros((8 - BATCH, D_MODEL), F32), c_sample], axis=0)
    mod_all = _ada_mod(c_all, w_ada, b_ada)
    mod_tok = jnp.stack([jnp.tile(mod_all[:, :BATCH], (1, MOD_ROWS // BATCH, 1)), mod_all[:, 8:]], axis=1)

    ab_re, ab_im, bb_re, bb_im = _discretise(lam_re, lam_im, log_dt, b_re, b_im)
    bcat, ccat, abr, abi = _block_diag_params(ab_re, ab_im, bb_re, bb_im, c_re, c_im)

    eye_b = jnp.eye(BATCH, dtype=F32)
    wk = (w_s[:, :, :, None, :, None] * eye_b[None, None, None, :, None, :]).reshape(
        DEPTH, N_GMLP_HEADS, TM_MIX, TM_MIX).astype(BF16)
    bsr = jnp.repeat(b_s.transpose(0, 2, 1), BATCH, axis=1)
    ws4 = jnp.repeat(w_s[:, :, :DEC_SEQ, :DEC_SEQ].transpose(0, 2, 3, 1).reshape(
        DEPTH, DEC_SEQ * DEC_SEQ, N_GMLP_HEADS), GMLP_HEAD, axis=2)
    bs4 = jnp.repeat(b_s[:, :, :DEC_SEQ].transpose(0, 2, 1), GMLP_HEAD, axis=2)
    wr_pad = jnp.pad(w_router, ((0, 0), (0, 0), (0, LANES - N_EXPERTS)))
    br_pad = jnp.pad(b_router, ((0, 0), (0, LANES - N_EXPERTS)), constant_values=-1e30).reshape(DEPTH, 1, LANES)

    h0_re = state_ssm_re.reshape(DEPTH, DEC_BATCH, N_SSM_GROUPS * SSM_STATE)
    h0_im = state_ssm_im.reshape(DEPTH, DEC_BATCH, N_SSM_GROUPS * SSM_STATE)

    st_p_re, st_p_im, st_s_re, st_s_im, v_rows = [], [], [], [], []
    for l in range(DEPTH):
        mod = mod_tok[l]
        u_a, gu_b, vn = _mix_in(x, mod, g_mix[l], w_in[l].astype(BF16), ln_v_g[l], ln_v_b[l])
        wglu_bf = w_glu[l].astype(BF16)
        ya_p, hp_re, hp_im = _s5_prompt(u_a, bcat[l], ccat[l], abr[l], abi[l], d_skip[l], wglu_bf, b_glu[l])
        ya_s, hs_re, hs_im = _s5_sample(u_a, h0_re[l], h0_im[l], bcat[l], ccat[l], abr[l], abi[l], d_skip[l],
                                        wglu_bf, b_glu[l])
        y_a = jnp.concatenate([ya_p, ya_s], axis=0)
        y_b = _gmlp(gu_b, vn, wk[l], bsr[l], ws4[l], bs4[l])
        x_new, h2_rows, idx, gates = _mix_out(
            x, y_a, y_b, g_out_a[l], g_out_b[l], w_out[l].astype(BF16), mod, g_ffn[l], wr_pad[l], br_pad[l])
        dest, te, nv, tb, zrow = _route(idx[:, :TOP_K])
        xs_rows = _gather_rows(dest, zrow, h2_rows)
        y_rows = _experts(l, te, nv, tb, xs_rows, w1, b1, w2, b2)
        x = _combine(dest, y_rows, x_new, gates, mod, g_final, l == DEPTH - 1)

        unblock = lambda h: h[:, :BATCH].transpose(1, 0, 2).reshape(BATCH, N_SSM_GROUPS, SSM_STATE)
        st_p_re.append(unblock(hp_re))
        st_p_im.append(unblock(hp_im))
        st_s_re.append(hs_re.reshape(DEC_BATCH, N_SSM_GROUPS, SSM_STATE))
        st_s_im.append(hs_im.reshape(DEC_BATCH, N_SSM_GROUPS, SSM_STATE))
        v_rows.append(vn[N_P:].reshape(DEC_SEQ, DEC_BATCH, D_GMLP).transpose(1, 0, 2))

    y_prompt = x[:N_P].reshape(SEQ, BATCH, D_MODEL).transpose(1, 0, 2)
    y_sample = x[N_P:].reshape(DEC_SEQ, DEC_BATCH, D_MODEL).transpose(1, 0, 2)
    return (y_prompt, y_sample, jnp.stack(st_p_re), jnp.stack(st_p_im),
            jnp.stack(st_s_re), jnp.stack(st_s_im), jnp.stack(v_rows))
```

```python
import functools

import jax
import jax.numpy as jnp
from jax import lax
from jax.experimental import pallas as pl
from jax.experimental.pallas import tpu as pltpu

F32 = jnp.float32
BF16 = jnp.bfloat16
I32 = jnp.int32

D_MODEL = 2048
BATCH = 4
SEQ = 2048
DEPTH = 4
DEC_BATCH = 128
DEC_SEQ = 4
D_SSM = 1024
SSM_GROUP = 16
N_SSM_GROUPS = 64
SSM_STATE = 64
D_GMLP = 1024
GMLP_HEAD = 128
N_GMLP_HEADS = 8
CHUNK = 128
D_IN = D_SSM + 2 * D_GMLP
N_EXPERTS = 32
TOP_K = 4
D_FF = D_MODEL
SWIGLU_ALPHA = 1.702
SWIGLU_LIMIT = 7.0
N_MOD = 6
EPS = 1e-5

LANES = 128
SUBLANES = 8
N_P = BATCH * SEQ
N_S = DEC_BATCH * DEC_SEQ
N_TOK = N_P + N_S
N_ASSIGN = N_TOK * TOP_K
C_ROWS = 8 + DEC_BATCH
MOD_ROWS = 128

TM = 256
TM_S5 = 512
TM_MIX = CHUNK * BATCH
TM_COMB = 128
ROW_TILES = D_MODEL // LANES
ROW_PITCH = ROW_TILES + SUBLANES

GROUP_BLOCK = 16
N_GB = N_SSM_GROUPS // GROUP_BLOCK
GB_CH = GROUP_BLOCK * SSM_GROUP
GB_ST = GROUP_BLOCK * SSM_STATE
SCAN_LANES = 512

ESB = 256
N_ESB = 5
ETM = N_ESB * ESB
ESB_TAIL = 128
SUB_BLOCKS = ([(i * ESB, ESB) for i in range(N_ESB - 1)]
              + [(ETM - ESB + i * ESB_TAIL, ESB_TAIL) for i in range(ESB // ESB_TAIL)])
N_ETILES = N_ASSIGN // ETM + N_EXPERTS
E_ROWS = N_ETILES * ETM
TF = 512
NJ1 = D_FF // TF
TN2 = 512
NJ2 = D_MODEL // TN2
N_COMB = TM_COMB * TOP_K

VMEM_LIMIT = 56 * 1024 * 1024


def _cparams(*sem):
    return pltpu.CompilerParams(dimension_semantics=sem, vmem_limit_bytes=VMEM_LIMIT)


def _const_spec(shape):
    zeros = (0,) * len(shape)
    return pl.BlockSpec(shape, lambda *_: zeros, pipeline_mode=pl.Buffered(1))


def _rms(x):
    return x * lax.rsqrt(jnp.mean(x * x, axis=-1, keepdims=True) + EPS)


def _modulate(xn, sc_ref, sh_ref):
    rows, dim = xn.shape
    x3 = xn.reshape(rows // MOD_ROWS, MOD_ROWS, dim)
    return (x3 * (1.0 + sc_ref[0][None]) + sh_ref[0][None]).reshape(rows, dim)


def _gate(v, g_ref):
    rows, dim = v.shape
    return (v.reshape(rows // MOD_ROWS, MOD_ROWS, dim) * g_ref[0][None]).reshape(rows, dim)


def _ada_kernel(c_ref, w_ref, b_ref, o_ref):
    s = jax.nn.silu(c_ref[...]).astype(BF16)
    o_ref[0] = jnp.dot(s, w_ref[0].astype(BF16), preferred_element_type=F32) + b_ref[0]


def _ada_mod(c_all, w_ada, b_ada):
    tn = 1024
    nj = N_MOD * D_MODEL // tn
    return pl.pallas_call(
        _ada_kernel,
        grid=(DEPTH, nj),
        in_specs=[
            pl.BlockSpec((C_ROWS, D_MODEL), lambda l, j: (0, 0)),
            pl.BlockSpec((1, D_MODEL, tn), lambda l, j: (l, 0, j)),
            pl.BlockSpec((1, 1, tn), lambda l, j: (l, 0, j)),
        ],
        out_specs=pl.BlockSpec((1, C_ROWS, tn), lambda l, j: (l, 0, j)),
        out_shape=jax.ShapeDtypeStruct((DEPTH, C_ROWS, N_MOD * D_MODEL), F32),
        compiler_params=_cparams("arbitrary", "arbitrary"),
        name="ada_mod",
    )(c_all, w_ada, b_ada.reshape(DEPTH, 1, N_MOD * D_MODEL))


def _disc_kernel(lr_ref, li_ref, ldt_ref, br_ref, bi_ref, abr_ref, abi_ref, bbr_ref, bbi_ref):
    lr = lr_ref[...]
    li = li_ref[...]
    dt = jnp.exp(ldt_ref[...])
    mag = jnp.exp(lr * dt)
    ab_re = mag * jnp.cos(li * dt)
    ab_im = mag * jnp.sin(li * dt)
    den = lr * lr + li * li
    q_re = ((ab_re - 1.0) * lr + ab_im * li) / den
    q_im = (ab_im * lr - (ab_re - 1.0) * li) / den
    br = br_ref[...]
    bi = bi_ref[...]
    abr_ref[...] = ab_re
    abi_ref[...] = ab_im
    bbr_ref[...] = q_re * br - q_im * bi
    bbi_ref[...] = q_re * bi + q_im * br


def _discretise(lam_re, lam_im, log_dt, b_re, b_im):
    rows = DEPTH * N_SSM_GROUPS
    cols = SSM_STATE * SSM_GROUP
    rep = lambda a: jnp.repeat(a.reshape(rows, SSM_STATE), SSM_GROUP, axis=1)
    full = pl.BlockSpec((rows, cols), lambda: (0, 0))
    out = jax.ShapeDtypeStruct((rows, cols), F32)
    return pl.pallas_call(
        _disc_kernel,
        in_specs=[full, full, pl.BlockSpec((rows, 1), lambda: (0, 0)), full, full],
        out_specs=[full] * 4,
        out_shape=[out] * 4,
        name="s5_discretise",
    )(rep(lam_re), rep(lam_im), log_dt.reshape(rows, 1),
      b_re.reshape(rows, cols), b_im.reshape(rows, cols))


def _block_diag_params(ab_re, ab_im, bb_re, bb_im, c_re, c_im):
    eye = jnp.eye(GROUP_BLOCK, dtype=F32)
    shp = (DEPTH, N_GB, GROUP_BLOCK, SSM_STATE, SSM_GROUP)

    def in_mat(bb):
        m = bb.reshape(shp)[:, :, :, None, :, :] * eye[None, None, :, :, None, None]
        return m.transpose(0, 1, 2, 5, 3, 4).reshape(DEPTH, N_GB, GB_CH, GB_ST)

    def out_mat(cc):
        c5 = cc.reshape(DEPTH, N_GB, GROUP_BLOCK, SSM_GROUP, SSM_STATE)
        m = c5[:, :, :, None, :, :] * eye[None, None, :, :, None, None]
        return m.transpose(0, 1, 2, 5, 3, 4).reshape(DEPTH, N_GB, GB_ST, GB_CH)

    bcat = jnp.concatenate([in_mat(bb_re), in_mat(bb_im)], axis=-1).astype(BF16)
    ccat = jnp.concatenate([out_mat(c_re), -out_mat(c_im)], axis=-2).astype(BF16)
    take = lambda a: a.reshape(DEPTH, N_GB, GROUP_BLOCK, SSM_STATE, SSM_GROUP)[..., 0].reshape(DEPTH, N_GB, 1, GB_ST)
    return bcat, ccat, take(ab_re), take(ab_im)


def _mix_in_kernel(x_ref, sh_ref, sc_ref, g_ref, w_ref, lng_ref, lnb_ref, ua_ref, gub_ref, vn_ref):
    xn = _rms(x_ref[...]) * g_ref[...]
    h = _modulate(xn, sc_ref, sh_ref)
    proj = jnp.dot(h.astype(BF16), w_ref[...], preferred_element_type=F32)
    ua_ref[...] = proj[:, :D_SSM]
    gub_ref[...] = jax.nn.gelu(proj[:, D_SSM:D_SSM + D_GMLP])
    gv = jax.nn.gelu(proj[:, D_SSM + D_GMLP:])
    mu = jnp.mean(gv, axis=-1, keepdims=True)
    var = jnp.mean(jnp.square(gv - mu), axis=-1, keepdims=True)
    vn_ref[...] = (gv - mu) * lax.rsqrt(var + EPS) * lng_ref[...] + lnb_ref[...]


def _mix_in(x, mod, g_mix, w_in_bf, ln_g, ln_b):
    p_tiles = N_P // TM
    mod_spec = lambda m: pl.BlockSpec((1, MOD_ROWS, D_MODEL), lambda i: (i // p_tiles, 0, m))
    out = jax.ShapeDtypeStruct((N_TOK, D_SSM), F32)
    return pl.pallas_call(
        _mix_in_kernel,
        grid=(N_TOK // TM,),
        in_specs=[
            pl.BlockSpec((TM, D_MODEL), lambda i: (i, 0)),
            mod_spec(0), mod_spec(1),
            _const_spec((1, D_MODEL)),
            _const_spec((D_MODEL, D_IN)),
            _const_spec((1, D_GMLP)), _const_spec((1, D_GMLP)),
        ],
        out_specs=[pl.BlockSpec((TM, D_SSM), lambda i: (i, 0))] * 3,
        out_shape=[out] * 3,
        compiler_params=_cparams("arbitrary"),
        name="mix_in",
    )(x, mod, mod, g_mix.reshape(1, D_MODEL), w_in_bf, ln_g.reshape(1, D_GMLP), ln_b.reshape(1, D_GMLP))


def _cmul_add(ar, ai, hr, hi, vr, vi):
    return ar * hr - ai * hi + vr, ar * hi + ai * hr + vi


def _s5_glu(y_s, wglu_ref, bglu_ref, ya_ref):
    ya = jax.nn.gelu(y_s[...])
    z = jnp.dot(ya.astype(BF16), wglu_ref[...], preferred_element_type=F32) + bglu_ref[...]
    ya_ref[...] = ya * jax.nn.sigmoid(z)


def _s5_prompt_kernel(u_ref, bcat_ref, ccat_ref, abr_ref, abi_ref, dsk_ref, wglu_ref, bglu_ref,
                      ya_ref, hre_ref, him_ref, bu_s, hs_s, y_s):
    @pl.when(pl.program_id(0) == 0)
    def _():
        hre_ref[...] = jnp.zeros_like(hre_ref)
        him_ref[...] = jnp.zeros_like(him_ref)

    u = u_ref[...]
    ub = u.astype(BF16)
    first_step = lax.broadcasted_iota(I32, (SUBLANES, SCAN_LANES), 0) < BATCH
    for gb in range(N_GB):
        cols = slice(gb * GB_CH, (gb + 1) * GB_CH)
        bu_s[...] = jnp.dot(ub[:, cols], bcat_ref[gb], preferred_element_type=F32)
        for lh in range(GB_ST // SCAN_LANES):
            re_c = slice(lh * SCAN_LANES, (lh + 1) * SCAN_LANES)
            im_c = slice(GB_ST + lh * SCAN_LANES, GB_ST + (lh + 1) * SCAN_LANES)
            ar = jnp.broadcast_to(abr_ref[gb, :, re_c], (SUBLANES, SCAN_LANES))
            ai = jnp.broadcast_to(abi_ref[gb, :, re_c], (SUBLANES, SCAN_LANES))

            def body(k, carry):
                hr, hi = carry
                rows = pl.ds(pl.multiple_of(k * SUBLANES, SUBLANES), SUBLANES)
                vr = bu_s[rows, re_c]
                vi = bu_s[rows, im_c]
                h1r, h1i = _cmul_add(ar, ai, hr, hi, vr, vi)
                h2r, h2i = _cmul_add(ar, ai, pltpu.roll(h1r, BATCH, 0), pltpu.roll(h1i, BATCH, 0), vr, vi)
                hs_s[rows, re_c] = jnp.where(first_step, h1r, h2r)
                hs_s[rows, im_c] = jnp.where(first_step, h1i, h2i)
                return pltpu.roll(h2r, BATCH, 0), pltpu.roll(h2i, BATCH, 0)

            hr, hi = lax.fori_loop(0, TM_S5 // SUBLANES, body, (hre_ref[gb, :, re_c], him_ref[gb, :, re_c]))
            hre_ref[gb, :, re_c] = hr
            him_ref[gb, :, re_c] = hi
        y = jnp.dot(hs_s[...].astype(BF16), ccat_ref[gb], preferred_element_type=F32)
        y_s[:, cols] = y + dsk_ref[:, cols] * u[:, cols]
    _s5_glu(y_s, wglu_ref, bglu_ref, ya_ref)


def _s5_sample_kernel(u_ref, h0r_ref, h0i_ref, bcat_ref, ccat_ref, abr_ref, abi_ref, dsk_ref, wglu_ref,
                      bglu_ref, ya_ref, hre_ref, him_ref, bu_s, hs_s, y_s):
    u = u_ref[...]
    ub = u.astype(BF16)
    for gb in range(N_GB):
        cols = slice(gb * GB_CH, (gb + 1) * GB_CH)
        bu_s[...] = jnp.dot(ub[:, cols], bcat_ref[gb], preferred_element_type=F32)
        for lh in range(GB_ST // SCAN_LANES):
            re_c = slice(lh * SCAN_LANES, (lh + 1) * SCAN_LANES)
            im_c = slice(GB_ST + lh * SCAN_LANES, GB_ST + (lh + 1) * SCAN_LANES)
            st_c = slice(gb * GB_ST + lh * SCAN_LANES, gb * GB_ST + (lh + 1) * SCAN_LANES)
            ar = jnp.broadcast_to(abr_ref[gb, :, re_c], (SUBLANES, SCAN_LANES))
            ai = jnp.broadcast_to(abi_ref[gb, :, re_c], (SUBLANES, SCAN_LANES))

            def body(q, carry):
                r0 = pl.multiple_of(q * SUBLANES, SUBLANES)
                hr = h0r_ref[pl.ds(r0, SUBLANES), st_c]
                hi = h0i_ref[pl.ds(r0, SUBLANES), st_c]
                for t in range(DEC_SEQ):
                    rows = pl.ds(t * DEC_BATCH + r0, SUBLANES)
                    hr, hi = _cmul_add(ar, ai, hr, hi, bu_s[rows, re_c], bu_s[rows, im_c])
                    hs_s[rows, re_c] = hr
                    hs_s[rows, im_c] = hi
                hre_ref[pl.ds(r0, SUBLANES), st_c] = hr
                him_ref[pl.ds(r0, SUBLANES), st_c] = hi
                return carry

            lax.fori_loop(0, DEC_BATCH // SUBLANES, body, 0)
        y = jnp.dot(hs_s[...].astype(BF16), ccat_ref[gb], preferred_element_type=F32)
        y_s[:, cols] = y + dsk_ref[:, cols] * u[:, cols]
    _s5_glu(y_s, wglu_ref, bglu_ref, ya_ref)


def _s5_common_specs():
    return [
        _const_spec((N_GB, GB_CH, 2 * GB_ST)),
        _const_spec((N_GB, 2 * GB_ST, GB_CH)),
        _const_spec((N_GB, 1, GB_ST)), _const_spec((N_GB, 1, GB_ST)),
        _const_spec((1, D_SSM)),
        _const_spec((D_SSM, D_SSM)),
        _const_spec((1, D_SSM)),
    ]


def _s5_prompt(u_a, bcat, ccat, abr, abi, d_skip, w_glu_bf, b_glu):
    st = jax.ShapeDtypeStruct((N_GB, SUBLANES, GB_ST), F32)
    st_spec = pl.BlockSpec((N_GB, SUBLANES, GB_ST), lambda i: (0, 0, 0))
    return pl.pallas_call(
        _s5_prompt_kernel,
        grid=(N_P // TM_S5,),
        in_specs=[pl.BlockSpec((TM_S5, D_SSM), lambda i: (i, 0))] + _s5_common_specs(),
        out_specs=[pl.BlockSpec((TM_S5, D_SSM), lambda i: (i, 0)), st_spec, st_spec],
        out_shape=[jax.ShapeDtypeStruct((N_P, D_SSM), F32), st, st],
        scratch_shapes=[pltpu.VMEM((TM_S5, 2 * GB_ST), F32), pltpu.VMEM((TM_S5, 2 * GB_ST), F32),
                        pltpu.VMEM((TM_S5, D_SSM), F32)],
        compiler_params=_cparams("arbitrary"),
        name="s5_prompt",
    )(u_a, bcat, ccat, abr, abi, d_skip.reshape(1, D_SSM), w_glu_bf, b_glu.reshape(1, D_SSM))


def _s5_sample(u_a, h0_re, h0_im, bcat, ccat, abr, abi, d_skip, w_glu_bf, b_glu):
    st = jax.ShapeDtypeStruct((DEC_BATCH, N_SSM_GROUPS * SSM_STATE), F32)
    st_spec = pl.BlockSpec((DEC_BATCH, N_SSM_GROUPS * SSM_STATE), lambda i: (0, 0))
    return pl.pallas_call(
        _s5_sample_kernel,
        grid=(1,),
        in_specs=[pl.BlockSpec((N_S, D_SSM), lambda i: (N_P // N_S, 0)), st_spec, st_spec] + _s5_common_specs(),
        out_specs=[pl.BlockSpec((N_S, D_SSM), lambda i: (0, 0)), st_spec, st_spec],
        out_shape=[jax.ShapeDtypeStruct((N_S, D_SSM), F32), st, st],
        scratch_shapes=[pltpu.VMEM((N_S, 2 * GB_ST), F32), pltpu.VMEM((N_S, 2 * GB_ST), F32),
                        pltpu.VMEM((N_S, D_SSM), F32)],
        compiler_params=_cparams("arbitrary"),
        name="s5_sample",
    )(u_a, h0_re, h0_im, bcat, ccat, abr, abi, d_skip.reshape(1, D_SSM), w_glu_bf, b_glu.reshape(1, D_SSM))


def _gmlp_kernel(gub_ref, vn_ref, wk_ref, bsr_ref, ws4_ref, bs4_ref, yb_ref):
    i = pl.program_id(0)

    @pl.when(i < N_P // TM_MIX)
    def _prompt_chunk():
        vb = vn_ref[...].astype(BF16)
        row = lax.broadcasted_iota(I32, (TM_MIX, TM_MIX), 0)
        col = lax.broadcasted_iota(I32, (TM_MIX, TM_MIX), 1)
        for h in range(N_GMLP_HEADS):
            hc = slice(h * GMLP_HEAD, (h + 1) * GMLP_HEAD)
            w = jnp.where(row >= col, wk_ref[h], jnp.zeros((), BF16))
            mix = jnp.dot(w, vb[:, hc], preferred_element_type=F32) + bsr_ref[:, h:h + 1]
            yb_ref[:, hc] = gub_ref[:, hc] * mix

    @pl.when(i == N_P // TM_MIX)
    def _sample_chunk():
        for t in range(DEC_SEQ):
            acc = jnp.broadcast_to(bs4_ref[t:t + 1, :], (DEC_BATCH, D_GMLP))
            for s in range(t + 1):
                w_ts = ws4_ref[t * DEC_SEQ + s:t * DEC_SEQ + s + 1, :]
                acc = acc + w_ts * vn_ref[s * DEC_BATCH:(s + 1) * DEC_BATCH, :]
            rows = slice(t * DEC_BATCH, (t + 1) * DEC_BATCH)
            yb_ref[rows, :] = gub_ref[rows, :] * acc


def _gmlp(gu_b, vn, wk, bsr, ws4, bs4):
    tok = pl.BlockSpec((TM_MIX, D_GMLP), lambda i: (i, 0))
    return pl.pallas_call(
        _gmlp_kernel,
        grid=(N_TOK // TM_MIX,),
        in_specs=[
            tok, tok,
            _const_spec((N_GMLP_HEADS, TM_MIX, TM_MIX)),
            _const_spec((TM_MIX, N_GMLP_HEADS)),
            _const_spec((DEC_SEQ * DEC_SEQ, D_GMLP)),
            _const_spec((DEC_SEQ, D_GMLP)),
        ],
        out_specs=tok,
        out_shape=jax.ShapeDtypeStruct((N_TOK, D_GMLP), F32),
        compiler_params=_cparams("arbitrary"),
        name="gmlp_mix",
    )(gu_b, vn, wk, bsr, ws4, bs4)


def _mix_out_kernel(x_ref, ya_ref, yb_ref, goa_ref, gob_ref, wout_ref, gm_ref, shf_ref, scf_ref, gffn_ref,
                    wr_ref, br_ref, xnew_ref, h2_ref, idx_ref, gate_ref):
    na = (_rms(ya_ref[...]) * goa_ref[...]).astype(BF16)
    nb = (_rms(yb_ref[...]) * gob_ref[...]).astype(BF16)
    o = (jnp.dot(na, wout_ref[:D_SSM, :], preferred_element_type=F32)
         + jnp.dot(nb, wout_ref[D_SSM:, :], preferred_element_type=F32))
    x_new = x_ref[...] + _gate(o, gm_ref)
    xnew_ref[...] = x_new

    h2 = _modulate(_rms(x_new) * gffn_ref[...], scf_ref, shf_ref)
    for lt in range(ROW_TILES):
        h2_ref[pl.ds(lt, TM, stride=ROW_TILES), :] = h2[:, lt * LANES:(lt + 1) * LANES]

    logits = jnp.dot(h2, wr_ref[...], preferred_element_type=F32, precision=lax.Precision.HIGHEST) + br_ref[...]
    lane = lax.broadcasted_iota(I32, (TM, LANES), 1)
    work = logits
    idx_out = jnp.zeros((TM, LANES), I32)
    exp_out = jnp.zeros((TM, LANES), F32)
    denom = jnp.zeros((TM, 1), F32)
    top = None
    for k in range(TOP_K):
        m = jnp.max(work, axis=-1, keepdims=True)
        sel = jnp.min(jnp.where(work == m, lane, LANES), axis=-1, keepdims=True)
        top = m if k == 0 else top
        e = jnp.exp(m - top)
        denom = denom + e
        idx_out = jnp.where(lane == k, sel, idx_out)
        exp_out = jnp.where(lane == k, e, exp_out)
        work = jnp.where(lane == sel, -jnp.inf, work)
    idx_ref[...] = idx_out
    gate_ref[...] = exp_out / denom


def _mix_out(x, y_a, y_b, g_out_a, g_out_b, w_out_bf, mod, g_ffn, wr_pad, br_pad):
    p_tiles = N_P // TM
    tok = lambda w: pl.BlockSpec((TM, w), lambda i: (i, 0))
    mod_spec = lambda m: pl.BlockSpec((1, MOD_ROWS, D_MODEL), lambda i: (i // p_tiles, 0, m))
    return pl.pallas_call(
        _mix_out_kernel,
        grid=(N_TOK // TM,),
        in_specs=[
            tok(D_MODEL), tok(D_SSM), tok(D_GMLP),
            _const_spec((1, D_SSM)), _const_spec((1, D_GMLP)),
            _const_spec((D_MODEL, D_MODEL)),
            mod_spec(2), mod_spec(3), mod_spec(4),
            _const_spec((1, D_MODEL)),
            _const_spec((D_MODEL, LANES)), _const_spec((1, LANES)),
        ],
        out_specs=[
            tok(D_MODEL),
            pl.BlockSpec((TM * ROW_TILES, LANES), lambda i: (i, 0)),
            tok(LANES), tok(LANES),
        ],
        out_shape=[
            jax.ShapeDtypeStruct((N_TOK, D_MODEL), F32),
            jax.ShapeDtypeStruct((N_TOK * ROW_TILES, LANES), F32),
            jax.ShapeDtypeStruct((N_TOK, LANES), I32),
            jax.ShapeDtypeStruct((N_TOK, LANES), F32),
        ],
        compiler_params=_cparams("arbitrary"),
        name="mix_out_router",
    )(x, y_a, y_b, g_out_a.reshape(1, D_SSM), g_out_b.reshape(1, D_GMLP),
      w_out_bf, mod, mod, mod, g_ffn.reshape(1, D_MODEL), wr_pad, br_pad)


def _route(idx):
    flat_e = idx.reshape(-1)
    onehot = (flat_e[:, None] == jnp.arange(N_EXPERTS, dtype=I32)[None, :]).astype(I32)
    csum = jnp.cumsum(onehot, axis=0)
    rank = jnp.take_along_axis(csum, flat_e[:, None], axis=1)[:, 0] - 1
    counts = csum[-1]
    ntiles = (counts + ETM - 1) // ETM
    tile_end = jnp.cumsum(ntiles)
    tile0 = tile_end - ntiles
    dest = tile0[flat_e] * ETM + rank
    n_used = tile_end[-1]
    t = jnp.arange(N_ETILES, dtype=I32)
    tb = jnp.minimum(t, n_used - 1)
    te = jnp.minimum(jnp.sum((tile_end[None, :] <= tb[:, None]).astype(I32), axis=1), N_EXPERTS - 1)
    nv = jnp.where(t < n_used, jnp.clip(counts[te] - (t - tile0[te]) * ETM, 0, ETM), 0).astype(I32)
    full = counts // ESB * ESB
    zrow = jnp.where(counts % ESB != 0, tile0 * ETM + full, -1).astype(I32)
    return dest.astype(I32), te, nv, tb, zrow


def _row_start(row):
    return row * ROW_TILES if isinstance(row, int) else pl.multiple_of(row * ROW_TILES, ROW_TILES)


def _row_copy(src_hbm, src_row, dst_hbm, dst_row, n_rows, sem):
    return pltpu.make_async_copy(
        src_hbm.at[pl.ds(_row_start(src_row), n_rows * ROW_TILES)],
        dst_hbm.at[pl.ds(_row_start(dst_row), n_rows * ROW_TILES)],
        sem)


def _gather_kernel(zrow_ref, dest_ref, h2_ref, xs_hbm, zeros_v, sem_z, sem):
    @pl.when(pl.program_id(0) == 0)
    def _zero_partial_subblocks():
        zeros_v[...] = jnp.zeros_like(zeros_v)
        for e in range(N_EXPERTS):
            @pl.when(zrow_ref[e] >= 0)
            def _():
                _row_copy(zeros_v, 0, xs_hbm, zrow_ref[e], ESB, sem_z).start()
        for e in range(N_EXPERTS):
            @pl.when(zrow_ref[e] >= 0)
            def _():
                _row_copy(zeros_v, 0, xs_hbm, zrow_ref[e], ESB, sem_z).wait()

    def body(tl, carry):
        for k in range(TOP_K):
            _row_copy(h2_ref, tl, xs_hbm, dest_ref[0, 0, tl * TOP_K + k], 1, sem).start()
        return carry

    lax.fori_loop(0, TM_COMB, body, 0, unroll=2)
    for _ in range(TOP_K):
        _row_copy(h2_ref, 0, xs_hbm, 0, TM_COMB, sem).wait()


def _gather_rows(dest, zrow, h2_rows):
    n_blk = N_TOK // TM_COMB
    return pl.pallas_call(
        _gather_kernel,
        grid_spec=pltpu.PrefetchScalarGridSpec(
            num_scalar_prefetch=1,
            grid=(n_blk,),
            in_specs=[
                pl.BlockSpec((1, 1, N_COMB), lambda i, z: (i, 0, 0), memory_space=pltpu.SMEM),
                pl.BlockSpec((TM_COMB * ROW_TILES, LANES), lambda i, z: (i, 0)),
            ],
            out_specs=pl.BlockSpec(memory_space=pl.ANY),
            scratch_shapes=[pltpu.VMEM((ESB * ROW_TILES, LANES), F32),
                            pltpu.SemaphoreType.DMA(()), pltpu.SemaphoreType.DMA(())],
        ),
        out_shape=jax.ShapeDtypeStruct((E_ROWS * ROW_TILES, LANES), F32),
        compiler_params=_cparams("arbitrary"),
        name="moe_gather",
    )(zrow, dest.reshape(n_blk, 1, N_COMB), h2_rows)


def _expert1_kernel(te_ref, nv_ref, tb_ref, x_ref, wg_ref, wl_ref, bg_ref, bl_ref, h_ref, xb_s):
    t = pl.program_id(0)
    j = pl.program_id(1)
    n = nv_ref[t]

    @pl.when(jnp.logical_and(j == 0, n > 0))
    def _to_matmul_layout():
        def to_matmul_layout(start, rows):
            for lt in range(ROW_TILES):
                v = x_ref[pl.ds(start * ROW_TILES + lt, rows, stride=ROW_TILES), :]
                xb_s[start:start + rows, lt * LANES:(lt + 1) * LANES] = v.astype(BF16)

        for start, rows in SUB_BLOCKS:
            pl.when(start < n)(functools.partial(to_matmul_layout, start, rows))

    @pl.when(n > 0)
    def _compute():
        wg = wg_ref[...].astype(BF16)
        wl = wl_ref[...].astype(BF16)

        def sub_block(start, rows):
            xs = xb_s[start:start + rows, :]
            glu = jnp.dot(xs, wg, preferred_element_type=F32) + bg_ref[...]
            lin = jnp.dot(xs, wl, preferred_element_type=F32) + bl_ref[...]
            glu = jnp.minimum(glu, SWIGLU_LIMIT)
            lin = jnp.clip(lin, -SWIGLU_LIMIT, SWIGLU_LIMIT)
            h_ref[start:start + rows, :] = (glu * jax.nn.sigmoid(SWIGLU_ALPHA * glu) * (lin + 1.0)).astype(BF16)

        sub_block(*SUB_BLOCKS[0])
        for start, rows in SUB_BLOCKS[1:]:
            pl.when(start < n)(functools.partial(sub_block, start, rows))


def _expert2_kernel(te_ref, nv_ref, tb_ref, h_ref, w_ref, b_ref, y_ref):
    t = pl.program_id(0)
    j = pl.program_id(1)
    n = nv_ref[t]

    @pl.when(n > 0)
    def _compute():
        w = w_ref[...].astype(BF16)

        def sub_block(start, rows):
            acc = jnp.dot(h_ref[start:start + rows, :], w, preferred_element_type=F32) + b_ref[...]
            for q in range(TN2 // LANES):
                first = start * ROW_TILES + j * (TN2 // LANES) + q
                y_ref[pl.ds(first, rows, stride=ROW_TILES), :] = acc[:, q * LANES:(q + 1) * LANES]

        sub_block(*SUB_BLOCKS[0])
        for start, rows in SUB_BLOCKS[1:]:
            pl.when(start < n)(functools.partial(sub_block, start, rows))


def _step(nv, t, j, last):
    return jnp.where(nv[t] > 0, j, last)


def _experts(layer, te, nv, tb, xs_rows, w1, b1, w2, b2):
    def w_spec(rows, cols, last, offset):
        return pl.BlockSpec((None, None, rows, cols),
                            lambda t, j, te, nv, tb: (layer, te[t], 0, offset + _step(nv, t, j, last)))

    b1r = b1.reshape(DEPTH, N_EXPERTS, 1, 2 * D_FF)
    h = pl.pallas_call(
        _expert1_kernel,
        grid_spec=pltpu.PrefetchScalarGridSpec(
            num_scalar_prefetch=3,
            grid=(N_ETILES, NJ1),
            in_specs=[
                pl.BlockSpec((ETM * ROW_TILES, LANES), lambda t, j, te, nv, tb: (tb[t], 0)),
                w_spec(D_MODEL, TF, NJ1 - 1, 0), w_spec(D_MODEL, TF, NJ1 - 1, NJ1),
                w_spec(1, TF, NJ1 - 1, 0), w_spec(1, TF, NJ1 - 1, NJ1),
            ],
            out_specs=pl.BlockSpec((ETM, TF), lambda t, j, te, nv, tb: (tb[t], _step(nv, t, j, NJ1 - 1))),
            scratch_shapes=[pltpu.VMEM((ETM, D_MODEL), BF16)],
        ),
        out_shape=jax.ShapeDtypeStruct((E_ROWS, D_FF), BF16),
        compiler_params=_cparams("arbitrary", "arbitrary"),
        name="moe_expert_in",
    )(te, nv, tb, xs_rows, w1, w1, b1r, b1r)
    return pl.pallas_call(
        _expert2_kernel,
        grid_spec=pltpu.PrefetchScalarGridSpec(
            num_scalar_prefetch=3,
            grid=(N_ETILES, NJ2),
            in_specs=[
                pl.BlockSpec((ETM, D_FF), lambda t, j, te, nv, tb: (tb[t], 0)),
                w_spec(D_FF, TN2, NJ2 - 1, 0), w_spec(1, TN2, NJ2 - 1, 0),
            ],
            out_specs=pl.BlockSpec((ETM * ROW_TILES, LANES), lambda t, j, te, nv, tb: (tb[t], 0)),
        ),
        out_shape=jax.ShapeDtypeStruct((E_ROWS * ROW_TILES, LANES), F32),
        compiler_params=_cparams("arbitrary", "arbitrary"),
        name="moe_expert_out",
    )(te, nv, tb, h, w2, b2.reshape(DEPTH, N_EXPERTS, 1, D_MODEL))


def _combine_kernel(final, dcur_ref, dnext_ref, y_hbm, x_ref, gate_ref, gf_ref, gfin_ref, o_ref, buf, sem):
    i = pl.program_id(0)
    n = pl.num_programs(0)
    slot = i % 2

    def issue(d_ref, s):
        def body(tl, carry):
            for k in range(TOP_K):
                src = d_ref[0, 0, tl * TOP_K + k]
                pltpu.make_async_copy(
                    y_hbm.at[pl.ds(pl.multiple_of(src * ROW_TILES, ROW_TILES), ROW_TILES)],
                    buf.at[s, pl.ds(pl.multiple_of((k * TM_COMB + tl) * ROW_PITCH, SUBLANES), ROW_TILES)],
                    sem.at[s]).start()
            return carry
        lax.fori_loop(0, TM_COMB, body, 0, unroll=2)

    @pl.when(i == 0)
    def _():
        issue(dcur_ref, 0)

    @pl.when(i + 1 < n)
    def _():
        issue(dnext_ref, 1 - slot)

    landed = buf.at[slot, pl.ds(0, N_COMB * ROW_TILES)]
    pltpu.make_async_copy(landed, landed, sem.at[slot]).wait()

    gates = gate_ref[...]
    x = x_ref[...]
    gf = gf_ref[0]
    pieces = []
    for lt in range(ROW_TILES):
        acc = jnp.zeros((TM_COMB, LANES), F32)
        for k in range(TOP_K):
            v = buf[slot, pl.ds(k * TM_COMB * ROW_PITCH + lt, TM_COMB, stride=ROW_PITCH), :]
            acc = acc + gates[:, k:k + 1] * v
        lc = slice(lt * LANES, (lt + 1) * LANES)
        pieces.append(x[:, lc] + gf[:, lc] * acc)
    out = jnp.concatenate(pieces, axis=-1)
    if final:
        out = _rms(out) * gfin_ref[...]
    o_ref[...] = out


def _combine(dest, y_rows, x_new, gates, mod, g_final, final):
    n_blk = N_TOK // TM_COMB
    p_tiles = N_P // TM_COMB
    d3 = dest.reshape(n_blk, 1, N_COMB)
    return pl.pallas_call(
        functools.partial(_combine_kernel, final),
        grid=(n_blk,),
        in_specs=[
            pl.BlockSpec((1, 1, N_COMB), lambda i: (i, 0, 0), memory_space=pltpu.SMEM),
            pl.BlockSpec((1, 1, N_COMB), lambda i: (jnp.minimum(i + 1, n_blk - 1), 0, 0), memory_space=pltpu.SMEM),
            pl.BlockSpec(memory_space=pl.ANY),
            pl.BlockSpec((TM_COMB, D_MODEL), lambda i: (i, 0)),
            pl.BlockSpec((TM_COMB, LANES), lambda i: (i, 0)),
            pl.BlockSpec((1, MOD_ROWS, D_MODEL), lambda i: (i // p_tiles, 0, 5)),
            _const_spec((1, D_MODEL)),
        ],
        out_specs=pl.BlockSpec((TM_COMB, D_MODEL), lambda i: (i, 0)),
        out_shape=jax.ShapeDtypeStruct((N_TOK, D_MODEL), F32),
        scratch_shapes=[pltpu.VMEM((2, N_COMB * ROW_PITCH, LANES), F32), pltpu.SemaphoreType.DMA((2,))],
        compiler_params=_cparams("arbitrary"),
        name="moe_combine",
    )(d3, d3, y_rows, x_new, gates, mod, g_final.reshape(1, D_MODEL))


def kernel(x_prompt, x_sample, c_prompt, c_sample, state_ssm_re, state_ssm_im, g_mix, w_ada, b_ada, w_in, lam_re, lam_im, log_dt, b_re, b_im, c_re, c_im, d_skip, w_glu, b_glu, ln_v_g, ln_v_b, w_s, b_s, g_out_a, g_out_b, w_out, g_ffn, w_router, b_router, w1, b1, w2, b2, g_final):
    x = jnp.concatenate([
        x_prompt.transpose(1, 0, 2).reshape(N_P, D_MODEL),
        x_sample.transpose(1, 0, 2).reshape(N_S, D_MODEL)], axis=0)

    c_all = jnp.concatenate([c_prompt, jnp.zeros((8 - BATCH, D_MODEL), F32), c_sample], axis=0)
    mod_all = _ada_mod(c_all, w_ada, b_ada)
    mod_tok = jnp.stack([jnp.tile(mod_all[:, :BATCH], (1, MOD_ROWS // BATCH, 1)), mod_all[:, 8:]], axis=1)

    ab_re, ab_im, bb_re, bb_im = _discretise(lam_re, lam_im, log_dt, b_re, b_im)
    bcat, ccat, abr, abi = _block_diag_params(ab_re, ab_im, bb_re, bb_im, c_re, c_im)

    r_idx = jnp.arange(TM_MIX, dtype=I32)
    spread = (r_idx[:, None] // BATCH == jnp.arange(CHUNK, dtype=I32)[None, :]).astype(F32)
    wk = jnp.einsum("rt,lhts,cs->lhrc", spread, w_s, spread, precision=lax.Precision.HIGHEST)
    wk = jnp.where(r_idx[:, None] % BATCH == r_idx[None, :] % BATCH, wk, 0.0).astype(BF16)
    bsr = jnp.repeat(b_s.transpose(0, 2, 1), BATCH, axis=1)
    ws4 = jnp.repeat(w_s[:, :, :DEC_SEQ, :DEC_SEQ].transpose(0, 2, 3, 1).reshape(
        DEPTH, DEC_SEQ * DEC_SEQ, N_GMLP_HEADS), GMLP_HEAD, axis=2)
    bs4 = jnp.repeat(b_s[:, :, :DEC_SEQ].transpose(0, 2, 1), GMLP_HEAD, axis=2)
    wr_pad = jnp.pad(w_router, ((0, 0), (0, 0), (0, LANES - N_EXPERTS)))
    br_pad = jnp.pad(b_router, ((0, 0), (0, LANES - N_EXPERTS)), constant_values=-1e30).reshape(DEPTH, 1, LANES)

    h0_re = state_ssm_re.reshape(DEPTH, DEC_BATCH, N_SSM_GROUPS * SSM_STATE)
    h0_im = state_ssm_im.reshape(DEPTH, DEC_BATCH, N_SSM_GROUPS * SSM_STATE)

    st_p_re, st_p_im, st_s_re, st_s_im, v_rows = [], [], [], [], []
    for l in range(DEPTH):
        mod = mod_tok[l]
        u_a, gu_b, vn = _mix_in(x, mod, g_mix[l], w_in[l].astype(BF16), ln_v_g[l], ln_v_b[l])
        wglu_bf = w_glu[l].astype(BF16)
        ya_p, hp_re, hp_im = _s5_prompt(u_a, bcat[l], ccat[l], abr[l], abi[l], d_skip[l], wglu_bf, b_glu[l])
        ya_s, hs_re, hs_im = _s5_sample(u_a, h0_re[l], h0_im[l], bcat[l], ccat[l], abr[l], abi[l], d_skip[l],
                                        wglu_bf, b_glu[l])
        y_a = jnp.concatenate([ya_p, ya_s], axis=0)
        y_b = _gmlp(gu_b, vn, wk[l], bsr[l], ws4[l], bs4[l])
        x_new, h2_rows, idx, gates = _mix_out(
            x, y_a, y_b, g_out_a[l], g_out_b[l], w_out[l].astype(BF16), mod, g_ffn[l], wr_pad[l], br_pad[l])
        dest, te, nv, tb, zrow = _route(idx[:, :TOP_K])
        xs_rows = _gather_rows(dest, zrow, h2_rows)
        y_rows = _experts(l, te, nv, tb, xs_rows, w1, b1, w2, b2)
        x = _combine(dest, y_rows, x_new, gates, mod, g_final, l == DEPTH - 1)

        unblock = lambda h: h[:, :BATCH].transpose(1, 0, 2).reshape(BATCH, N_SSM_GROUPS, SSM_STATE)
        st_p_re.append(unblock(hp_re))
        st_p_im.append(unblock(hp_im))
        st_s_re.append(hs_re.reshape(DEC_BATCH, N_SSM_GROUPS, SSM_STATE))
        st_s_im.append(hs_im.reshape(DEC_BATCH, N_SSM_GROUPS, SSM_STATE))
        v_rows.append(vn[N_P:].reshape(DEC_SEQ, DEC_BATCH, D_GMLP).transpose(1, 0, 2))

    y_prompt = x[:N_P].reshape(SEQ, BATCH, D_MODEL).transpose(1, 0, 2)
    y_sample = x[N_P:].reshape(DEC_SEQ, DEC_BATCH, D_MODEL).transpose(1, 0, 2)
    return (y_prompt, y_sample, jnp.stack(st_p_re), jnp.stack(st_p_im),
            jnp.stack(st_s_re), jnp.stack(st_s_im), jnp.stack(v_rows))
```

```python
import functools

import jax
import jax.numpy as jnp
from jax import lax
from jax.experimental import pallas as pl
from jax.experimental.pallas import tpu as pltpu

F32 = jnp.float32
BF16 = jnp.bfloat16
I32 = jnp.int32

D_MODEL = 2048
BATCH = 4
SEQ = 2048
DEPTH = 4
DEC_BATCH = 128
DEC_SEQ = 4
D_SSM = 1024
SSM_GROUP = 16
N_SSM_GROUPS = 64
SSM_STATE = 64
D_GMLP = 1024
GMLP_HEAD = 128
N_GMLP_HEADS = 8
CHUNK = 128
D_IN = D_SSM + 2 * D_GMLP
N_EXPERTS = 32
TOP_K = 4
D_FF = D_MODEL
SWIGLU_ALPHA = 1.702
SWIGLU_LIMIT = 7.0
N_MOD = 6
EPS = 1e-5

LANES = 128
SUBLANES = 8
N_P = BATCH * SEQ
N_S = DEC_BATCH * DEC_SEQ
N_TOK = N_P + N_S
N_ASSIGN = N_TOK * TOP_K
C_ROWS = 8 + DEC_BATCH
MOD_ROWS = 128

TM = 256
TM_S5 = 512
TM_MIX = CHUNK * BATCH
TM_COMB = 128
ROW_TILES = D_MODEL // LANES
ROW_PITCH = ROW_TILES + SUBLANES

GROUP_BLOCK = 16
N_GB = N_SSM_GROUPS // GROUP_BLOCK
GB_CH = GROUP_BLOCK * SSM_GROUP
GB_ST = GROUP_BLOCK * SSM_STATE
SCAN_LANES = 512

ESB = 256
N_ESB = 5
ETM = N_ESB * ESB
ESB_TAIL = 128
SUB_BLOCKS = ([(i * ESB, ESB) for i in range(N_ESB - 1)]
              + [(ETM - ESB + i * ESB_TAIL, ESB_TAIL) for i in range(ESB // ESB_TAIL)])
N_ETILES = N_ASSIGN // ETM + N_EXPERTS
E_ROWS = N_ETILES * ETM
TF = 512
NJ1 = D_FF // TF
TN2 = 512
NJ2 = D_MODEL // TN2
N_COMB = TM_COMB * TOP_K

VMEM_LIMIT = 56 * 1024 * 1024


def _cparams(*sem):
    return pltpu.CompilerParams(dimension_semantics=sem, vmem_limit_bytes=VMEM_LIMIT)


def _const_spec(shape):
    zeros = (0,) * len(shape)
    return pl.BlockSpec(shape, lambda *_: zeros, pipeline_mode=pl.Buffered(1))


def _rms(x):
    return x * lax.rsqrt(jnp.mean(x * x, axis=-1, keepdims=True) + EPS)


def _modulate(xn, sc_ref, sh_ref):
    rows, dim = xn.shape
    x3 = xn.reshape(rows // MOD_ROWS, MOD_ROWS, dim)
    return (x3 * (1.0 + sc_ref[0][None]) + sh_ref[0][None]).reshape(rows, dim)


def _gate(v, g_ref):
    rows, dim = v.shape
    return (v.reshape(rows // MOD_ROWS, MOD_ROWS, dim) * g_ref[0][None]).reshape(rows, dim)


def _ada_kernel(c_ref, w_ref, b_ref, o_ref):
    s = jax.nn.silu(c_ref[...]).astype(BF16)
    o_ref[0] = jnp.dot(s, w_ref[0].astype(BF16), preferred_element_type=F32) + b_ref[0]


def _ada_mod(c_all, w_ada, b_ada):
    tn = 1024
    nj = N_MOD * D_MODEL // tn
    return pl.pallas_call(
        _ada_kernel,
        grid=(DEPTH, nj),
        in_specs=[
            pl.BlockSpec((C_ROWS, D_MODEL), lambda l, j: (0, 0)),
            pl.BlockSpec((1, D_MODEL, tn), lambda l, j: (l, 0, j)),
            pl.BlockSpec((1, 1, tn), lambda l, j: (l, 0, j)),
        ],
        out_specs=pl.BlockSpec((1, C_ROWS, tn), lambda l, j: (l, 0, j)),
        out_shape=jax.ShapeDtypeStruct((DEPTH, C_ROWS, N_MOD * D_MODEL), F32),
        compiler_params=_cparams("arbitrary", "arbitrary"),
        name="ada_mod",
    )(c_all, w_ada, b_ada.reshape(DEPTH, 1, N_MOD * D_MODEL))


def _disc_kernel(lr_ref, li_ref, ldt_ref, br_ref, bi_ref, abr_ref, abi_ref, bbr_ref, bbi_ref):
    lr = lr_ref[...]
    li = li_ref[...]
    dt = jnp.exp(ldt_ref[...])
    mag = jnp.exp(lr * dt)
    ab_re = mag * jnp.cos(li * dt)
    ab_im = mag * jnp.sin(li * dt)
    den = lr * lr + li * li
    q_re = ((ab_re - 1.0) * lr + ab_im * li) / den
    q_im = (ab_im * lr - (ab_re - 1.0) * li) / den
    br = br_ref[...]
    bi = bi_ref[...]
    abr_ref[...] = ab_re
    abi_ref[...] = ab_im
    bbr_ref[...] = q_re * br - q_im * bi
    bbi_ref[...] = q_re * bi + q_im * br


def _discretise(lam_re, lam_im, log_dt, b_re, b_im):
    rows = DEPTH * N_SSM_GROUPS
    cols = SSM_STATE * SSM_GROUP
    rep = lambda a: jnp.repeat(a.reshape(rows, SSM_STATE), SSM_GROUP, axis=1)
    full = pl.BlockSpec((rows, cols), lambda: (0, 0))
    out = jax.ShapeDtypeStruct((rows, cols), F32)
    return pl.pallas_call(
        _disc_kernel,
        in_specs=[full, full, pl.BlockSpec((rows, 1), lambda: (0, 0)), full, full],
        out_specs=[full] * 4,
        out_shape=[out] * 4,
        name="s5_discretise",
    )(rep(lam_re), rep(lam_im), log_dt.reshape(rows, 1),
      b_re.reshape(rows, cols), b_im.reshape(rows, cols))


def _block_diag_params(ab_re, ab_im, bb_re, bb_im, c_re, c_im):
    eye = jnp.eye(GROUP_BLOCK, dtype=F32)
    shp = (DEPTH, N_GB, GROUP_BLOCK, SSM_STATE, SSM_GROUP)

    def in_mat(bb):
        m = bb.reshape(shp)[:, :, :, None, :, :] * eye[None, None, :, :, None, None]
        return m.transpose(0, 1, 2, 5, 3, 4).reshape(DEPTH, N_GB, GB_CH, GB_ST)

    def out_mat(cc):
        c5 = cc.reshape(DEPTH, N_GB, GROUP_BLOCK, SSM_GROUP, SSM_STATE)
        m = c5[:, :, :, None, :, :] * eye[None, None, :, :, None, None]
        return m.transpose(0, 1, 2, 5, 3, 4).reshape(DEPTH, N_GB, GB_ST, GB_CH)

    bcat = jnp.concatenate([in_mat(bb_re), in_mat(bb_im)], axis=-1).astype(BF16)
    ccat = jnp.concatenate([out_mat(c_re), -out_mat(c_im)], axis=-2).astype(BF16)
    take = lambda a: a.reshape(DEPTH, N_GB, GROUP_BLOCK, SSM_STATE, SSM_GROUP)[..., 0].reshape(DEPTH, N_GB, 1, GB_ST)
    return bcat, ccat, take(ab_re), take(ab_im)


def _mix_in_kernel(x_ref, sh_ref, sc_ref, g_ref, w_ref, lng_ref, lnb_ref, ua_ref, gub_ref, vn_ref):
    xn = _rms(x_ref[...]) * g_ref[...]
    h = _modulate(xn, sc_ref, sh_ref)
    proj = jnp.dot(h.astype(BF16), w_ref[...], preferred_element_type=F32)
    ua_ref[...] = proj[:, :D_SSM]
    gub_ref[...] = jax.nn.gelu(proj[:, D_SSM:D_SSM + D_GMLP])
    gv = jax.nn.gelu(proj[:, D_SSM + D_GMLP:])
    mu = jnp.mean(gv, axis=-1, keepdims=True)
    var = jnp.mean(jnp.square(gv - mu), axis=-1, keepdims=True)
    vn_ref[...] = (gv - mu) * lax.rsqrt(var + EPS) * lng_ref[...] + lnb_ref[...]


def _mix_in(x, mod, g_mix, w_in_bf, ln_g, ln_b):
    p_tiles = N_P // TM
    mod_spec = lambda m: pl.BlockSpec((1, MOD_ROWS, D_MODEL), lambda i: (i // p_tiles, 0, m))
    out = jax.ShapeDtypeStruct((N_TOK, D_SSM), F32)
    return pl.pallas_call(
        _mix_in_kernel,
        grid=(N_TOK // TM,),
        in_specs=[
            pl.BlockSpec((TM, D_MODEL), lambda i: (i, 0)),
            mod_spec(0), mod_spec(1),
            _const_spec((1, D_MODEL)),
            _const_spec((D_MODEL, D_IN)),
            _const_spec((1, D_GMLP)), _const_spec((1, D_GMLP)),
        ],
        out_specs=[pl.BlockSpec((TM, D_SSM), lambda i: (i, 0))] * 3,
        out_shape=[out] * 3,
        compiler_params=_cparams("arbitrary"),
        name="mix_in",
    )(x, mod, mod, g_mix.reshape(1, D_MODEL), w_in_bf, ln_g.reshape(1, D_GMLP), ln_b.reshape(1, D_GMLP))


def _cmul_add(ar, ai, hr, hi, vr, vi):
    return ar * hr - ai * hi + vr, ar * hi + ai * hr + vi


def _s5_glu(y_s, wglu_ref, bglu_ref, ya_ref):
    ya = jax.nn.gelu(y_s[...])
    z = jnp.dot(ya.astype(BF16), wglu_ref[...], preferred_element_type=F32) + bglu_ref[...]
    ya_ref[...] = ya * jax.nn.sigmoid(z)


def _s5_prompt_kernel(u_ref, bcat_ref, ccat_ref, abr_ref, abi_ref, dsk_ref, wglu_ref, bglu_ref,
                      ya_ref, hre_ref, him_ref, bu_s, hs_s, y_s):
    @pl.when(pl.program_id(0) == 0)
    def _():
        hre_ref[...] = jnp.zeros_like(hre_ref)
        him_ref[...] = jnp.zeros_like(him_ref)

    u = u_ref[...]
    ub = u.astype(BF16)
    first_step = lax.broadcasted_iota(I32, (SUBLANES, SCAN_LANES), 0) < BATCH
    for gb in range(N_GB):
        cols = slice(gb * GB_CH, (gb + 1) * GB_CH)
        bu_s[...] = jnp.dot(ub[:, cols], bcat_ref[gb], preferred_element_type=F32)
        for lh in range(GB_ST // SCAN_LANES):
            re_c = slice(lh * SCAN_LANES, (lh + 1) * SCAN_LANES)
            im_c = slice(GB_ST + lh * SCAN_LANES, GB_ST + (lh + 1) * SCAN_LANES)
            ar = jnp.broadcast_to(abr_ref[gb, :, re_c], (SUBLANES, SCAN_LANES))
            ai = jnp.broadcast_to(abi_ref[gb, :, re_c], (SUBLANES, SCAN_LANES))

            def body(k, carry):
                hr, hi = carry
                rows = pl.ds(pl.multiple_of(k * SUBLANES, SUBLANES), SUBLANES)
                vr = bu_s[rows, re_c]
                vi = bu_s[rows, im_c]
                h1r, h1i = _cmul_add(ar, ai, hr, hi, vr, vi)
                h2r, h2i = _cmul_add(ar, ai, pltpu.roll(h1r, BATCH, 0), pltpu.roll(h1i, BATCH, 0), vr, vi)
                hs_s[rows, re_c] = jnp.where(first_step, h1r, h2r)
                hs_s[rows, im_c] = jnp.where(first_step, h1i, h2i)
                return pltpu.roll(h2r, BATCH, 0), pltpu.roll(h2i, BATCH, 0)

            hr, hi = lax.fori_loop(0, TM_S5 // SUBLANES, body, (hre_ref[gb, :, re_c], him_ref[gb, :, re_c]))
            hre_ref[gb, :, re_c] = hr
            him_ref[gb, :, re_c] = hi
        y = jnp.dot(hs_s[...].astype(BF16), ccat_ref[gb], preferred_element_type=F32)
        y_s[:, cols] = y + dsk_ref[:, cols] * u[:, cols]
    _s5_glu(y_s, wglu_ref, bglu_ref, ya_ref)


def _s5_sample_kernel(u_ref, h0r_ref, h0i_ref, bcat_ref, ccat_ref, abr_ref, abi_ref, dsk_ref, wglu_ref,
                      bglu_ref, ya_ref, hre_ref, him_ref, bu_s, hs_s, y_s):
    u = u_ref[...]
    ub = u.astype(BF16)
    for gb in range(N_GB):
        cols = slice(gb * GB_CH, (gb + 1) * GB_CH)
        bu_s[...] = jnp.dot(ub[:, cols], bcat_ref[gb], preferred_element_type=F32)
        for lh in range(GB_ST // SCAN_LANES):
            re_c = slice(lh * SCAN_LANES, (lh + 1) * SCAN_LANES)
            im_c = slice(GB_ST + lh * SCAN_LANES, GB_ST + (lh + 1) * SCAN_LANES)
            st_c = slice(gb * GB_ST + lh * SCAN_LANES, gb * GB_ST + (lh + 1) * SCAN_LANES)
            ar = jnp.broadcast_to(abr_ref[gb, :, re_c], (SUBLANES, SCAN_LANES))
            ai = jnp.broadcast_to(abi_ref[gb, :, re_c], (SUBLANES, SCAN_LANES))

            def body(q, carry):
                r0 = pl.multiple_of(q * SUBLANES, SUBLANES)
                hr = h0r_ref[pl.ds(r0, SUBLANES), st_c]
                hi = h0i_ref[pl.ds(r0, SUBLANES), st_c]
                for t in range(DEC_SEQ):
                    rows = pl.ds(t * DEC_BATCH + r0, SUBLANES)
                    hr, hi = _cmul_add(ar, ai, hr, hi, bu_s[rows, re_c], bu_s[rows, im_c])
                    hs_s[rows, re_c] = hr
                    hs_s[rows, im_c] = hi
                hre_ref[pl.ds(r0, SUBLANES), st_c] = hr
                him_ref[pl.ds(r0, SUBLANES), st_c] = hi
                return carry

            lax.fori_loop(0, DEC_BATCH // SUBLANES, body, 0)
        y = jnp.dot(hs_s[...].astype(BF16), ccat_ref[gb], preferred_element_type=F32)
        y_s[:, cols] = y + dsk_ref[:, cols] * u[:, cols]
    _s5_glu(y_s, wglu_ref, bglu_ref, ya_ref)


def _s5_common_specs():
    return [
        _const_spec((N_GB, GB_CH, 2 * GB_ST)),
        _const_spec((N_GB, 2 * GB_ST, GB_CH)),
        _const_spec((N_GB, 1, GB_ST)), _const_spec((N_GB, 1, GB_ST)),
        _const_spec((1, D_SSM)),
        _const_spec((D_SSM, D_SSM)),
        _const_spec((1, D_SSM)),
    ]


def _s5_prompt(u_a, bcat, ccat, abr, abi, d_skip, w_glu_bf, b_glu):
    st = jax.ShapeDtypeStruct((N_GB, SUBLANES, GB_ST), F32)
    st_spec = pl.BlockSpec((N_GB, SUBLANES, GB_ST), lambda i: (0, 0, 0))
    return pl.pallas_call(
        _s5_prompt_kernel,
        grid=(N_P // TM_S5,),
        in_specs=[pl.BlockSpec((TM_S5, D_SSM), lambda i: (i, 0))] + _s5_common_specs(),
        out_specs=[pl.BlockSpec((TM_S5, D_SSM), lambda i: (i, 0)), st_spec, st_spec],
        out_shape=[jax.ShapeDtypeStruct((N_P, D_SSM), F32), st, st],
        scratch_shapes=[pltpu.VMEM((TM_S5, 2 * GB_ST), F32), pltpu.VMEM((TM_S5, 2 * GB_ST), F32),
                        pltpu.VMEM((TM_S5, D_SSM), F32)],
        compiler_params=_cparams("arbitrary"),
        name="s5_prompt",
    )(u_a, bcat, ccat, abr, abi, d_skip.reshape(1, D_SSM), w_glu_bf, b_glu.reshape(1, D_SSM))


def _s5_sample(u_a, h0_re, h0_im, bcat, ccat, abr, abi, d_skip, w_glu_bf, b_glu):
    st = jax.ShapeDtypeStruct((DEC_BATCH, N_SSM_GROUPS * SSM_STATE), F32)
    st_spec = pl.BlockSpec((DEC_BATCH, N_SSM_GROUPS * SSM_STATE), lambda i: (0, 0))
    return pl.pallas_call(
        _s5_sample_kernel,
        grid=(1,),
        in_specs=[pl.BlockSpec((N_S, D_SSM), lambda i: (N_P // N_S, 0)), st_spec, st_spec] + _s5_common_specs(),
        out_specs=[pl.BlockSpec((N_S, D_SSM), lambda i: (0, 0)), st_spec, st_spec],
        out_shape=[jax.ShapeDtypeStruct((N_S, D_SSM), F32), st, st],
        scratch_shapes=[pltpu.VMEM((N_S, 2 * GB_ST), F32), pltpu.VMEM((N_S, 2 * GB_ST), F32),
                        pltpu.VMEM((N_S, D_SSM), F32)],
        compiler_params=_cparams("arbitrary"),
        name="s5_sample",
    )(u_a, h0_re, h0_im, bcat, ccat, abr, abi, d_skip.reshape(1, D_SSM), w_glu_bf, b_glu.reshape(1, D_SSM))


def _gmlp_kernel(gub_ref, vn_ref, wk_ref, bsr_ref, ws4_ref, bs4_ref, yb_ref):
    i = pl.program_id(0)

    @pl.when(i < N_P // TM_MIX)
    def _prompt_chunk():
        vb = vn_ref[...].astype(BF16)
        row = lax.broadcasted_iota(I32, (TM_MIX, TM_MIX), 0)
        col = lax.broadcasted_iota(I32, (TM_MIX, TM_MIX), 1)
        for h in range(N_GMLP_HEADS):
            hc = slice(h * GMLP_HEAD, (h + 1) * GMLP_HEAD)
            w = jnp.where(row >= col, wk_ref[h], jnp.zeros((), BF16))
            mix = jnp.dot(w, vb[:, hc], preferred_element_type=F32) + bsr_ref[:, h:h + 1]
            yb_ref[:, hc] = gub_ref[:, hc] * mix

    @pl.when(i == N_P // TM_MIX)
    def _sample_chunk():
        for t in range(DEC_SEQ):
            acc = jnp.broadcast_to(bs4_ref[t:t + 1, :], (DEC_BATCH, D_GMLP))
            for s in range(t + 1):
                w_ts = ws4_ref[t * DEC_SEQ + s:t * DEC_SEQ + s + 1, :]
                acc = acc + w_ts * vn_ref[s * DEC_BATCH:(s + 1) * DEC_BATCH, :]
            rows = slice(t * DEC_BATCH, (t + 1) * DEC_BATCH)
            yb_ref[rows, :] = gub_ref[rows, :] * acc


def _gmlp(gu_b, vn, wk, bsr, ws4, bs4):
    tok = pl.BlockSpec((TM_MIX, D_GMLP), lambda i: (i, 0))
    return pl.pallas_call(
        _gmlp_kernel,
        grid=(N_TOK // TM_MIX,),
        in_specs=[
            tok, tok,
            _const_spec((N_GMLP_HEADS, TM_MIX, TM_MIX)),
            _const_spec((TM_MIX, N_GMLP_HEADS)),
            _const_spec((DEC_SEQ * DEC_SEQ, D_GMLP)),
            _const_spec((DEC_SEQ, D_GMLP)),
        ],
        out_specs=tok,
        out_shape=jax.ShapeDtypeStruct((N_TOK, D_GMLP), F32),
        compiler_params=_cparams("arbitrary"),
        name="gmlp_mix",
    )(gu_b, vn, wk, bsr, ws4, bs4)


def _mix_out_kernel(x_ref, ya_ref, yb_ref, goa_ref, gob_ref, wout_ref, gm_ref, shf_ref, scf_ref, gffn_ref,
                    wr_ref, br_ref, xnew_ref, h2_ref, idx_ref, gate_ref):
    na = (_rms(ya_ref[...]) * goa_ref[...]).astype(BF16)
    nb = (_rms(yb_ref[...]) * gob_ref[...]).astype(BF16)
    o = (jnp.dot(na, wout_ref[:D_SSM, :], preferred_element_type=F32)
         + jnp.dot(nb, wout_ref[D_SSM:, :], preferred_element_type=F32))
    x_new = x_ref[...] + _gate(o, gm_ref)
    xnew_ref[...] = x_new

    h2 = _modulate(_rms(x_new) * gffn_ref[...], scf_ref, shf_ref)
    for lt in range(ROW_TILES):
        h2_ref[pl.ds(lt, TM, stride=ROW_TILES), :] = h2[:, lt * LANES:(lt + 1) * LANES]

    logits = jnp.dot(h2, wr_ref[...], preferred_element_type=F32, precision=lax.Precision.HIGHEST) + br_ref[...]
    lane = lax.broadcasted_iota(I32, (TM, LANES), 1)
    work = logits
    idx_out = jnp.zeros((TM, LANES), I32)
    exp_out = jnp.zeros((TM, LANES), F32)
    denom = jnp.zeros((TM, 1), F32)
    top = None
    for k in range(TOP_K):
        m = jnp.max(work, axis=-1, keepdims=True)
        sel = jnp.min(jnp.where(work == m, lane, LANES), axis=-1, keepdims=True)
        top = m if k == 0 else top
        e = jnp.exp(m - top)
        denom = denom + e
        idx_out = jnp.where(lane == k, sel, idx_out)
        exp_out = jnp.where(lane == k, e, exp_out)
        work = jnp.where(lane == sel, -jnp.inf, work)
    idx_ref[...] = idx_out
    gate_ref[...] = exp_out / denom


def _mix_out(x, y_a, y_b, g_out_a, g_out_b, w_out_bf, mod, g_ffn, wr_pad, br_pad):
    p_tiles = N_P // TM
    tok = lambda w: pl.BlockSpec((TM, w), lambda i: (i, 0))
    mod_spec = lambda m: pl.BlockSpec((1, MOD_ROWS, D_MODEL), lambda i: (i // p_tiles, 0, m))
    return pl.pallas_call(
        _mix_out_kernel,
        grid=(N_TOK // TM,),
        in_specs=[
            tok(D_MODEL), tok(D_SSM), tok(D_GMLP),
            _const_spec((1, D_SSM)), _const_spec((1, D_GMLP)),
            _const_spec((D_MODEL, D_MODEL)),
            mod_spec(2), mod_spec(3), mod_spec(4),
            _const_spec((1, D_MODEL)),
            _const_spec((D_MODEL, LANES)), _const_spec((1, LANES)),
        ],
        out_specs=[
            tok(D_MODEL),
            pl.BlockSpec((TM * ROW_TILES, LANES), lambda i: (i, 0)),
            tok(LANES), tok(LANES),
        ],
        out_shape=[
            jax.ShapeDtypeStruct((N_TOK, D_MODEL), F32),
            jax.ShapeDtypeStruct((N_TOK * ROW_TILES, LANES), F32),
            jax.ShapeDtypeStruct((N_TOK, LANES), I32),
            jax.ShapeDtypeStruct((N_TOK, LANES), F32),
        ],
        compiler_params=_cparams("arbitrary"),
        name="mix_out_router",
    )(x, y_a, y_b, g_out_a.reshape(1, D_SSM), g_out_b.reshape(1, D_GMLP),
      w_out_bf, mod, mod, mod, g_ffn.reshape(1, D_MODEL), wr_pad, br_pad)


def _route(idx):
    flat_e = idx.reshape(-1)
    onehot = (flat_e[:, None] == jnp.arange(N_EXPERTS, dtype=I32)[None, :]).astype(I32)
    csum = jnp.cumsum(onehot, axis=0)
    rank = jnp.take_along_axis(csum, flat_e[:, None], axis=1)[:, 0] - 1
    counts = csum[-1]
    ntiles = (counts + ETM - 1) // ETM
    tile_end = jnp.cumsum(ntiles)
    tile0 = tile_end - ntiles
    dest = tile0[flat_e] * ETM + rank
    n_used = tile_end[-1]
    t = jnp.arange(N_ETILES, dtype=I32)
    tb = jnp.minimum(t, n_used - 1)
    te = jnp.minimum(jnp.sum((tile_end[None, :] <= tb[:, None]).astype(I32), axis=1), N_EXPERTS - 1)
    nv = jnp.where(t < n_used, jnp.clip(counts[te] - (t - tile0[te]) * ETM, 0, ETM), 0).astype(I32)
    full = counts // ESB * ESB
    zrow = jnp.where(counts % ESB != 0, tile0 * ETM + full, -1).astype(I32)
    return dest.astype(I32), te, nv, tb, zrow


def _row_start(row):
    return row * ROW_TILES if isinstance(row, int) else pl.multiple_of(row * ROW_TILES, ROW_TILES)


def _row_copy(src_hbm, src_row, dst_hbm, dst_row, n_rows, sem):
    return pltpu.make_async_copy(
        src_hbm.at[pl.ds(_row_start(src_row), n_rows * ROW_TILES)],
        dst_hbm.at[pl.ds(_row_start(dst_row), n_rows * ROW_TILES)],
        sem)


def _gather_kernel(zrow_ref, dest_ref, h2_ref, xs_hbm, zeros_v, sem_z, sem):
    @pl.when(pl.program_id(0) == 0)
    def _zero_partial_subblocks():
        zeros_v[...] = jnp.zeros_like(zeros_v)
        for e in range(N_EXPERTS):
            @pl.when(zrow_ref[e] >= 0)
            def _():
                _row_copy(zeros_v, 0, xs_hbm, zrow_ref[e], ESB, sem_z).start()
        for e in range(N_EXPERTS):
            @pl.when(zrow_ref[e] >= 0)
            def _():
                _row_copy(zeros_v, 0, xs_hbm, zrow_ref[e], ESB, sem_z).wait()

    def body(tl, carry):
        for k in range(TOP_K):
            _row_copy(h2_ref, tl, xs_hbm, dest_ref[0, 0, tl * TOP_K + k], 1, sem).start()
        return carry

    lax.fori_loop(0, TM_COMB, body, 0, unroll=2)
    for _ in range(TOP_K):
        _row_copy(h2_ref, 0, xs_hbm, 0, TM_COMB, sem).wait()


def _gather_rows(dest, zrow, h2_rows):
    n_blk = N_TOK // TM_COMB
    return pl.pallas_call(
        _gather_kernel,
        grid_spec=pltpu.PrefetchScalarGridSpec(
            num_scalar_prefetch=1,
            grid=(n_blk,),
            in_specs=[
                pl.BlockSpec((1, 1, N_COMB), lambda i, z: (i, 0, 0), memory_space=pltpu.SMEM),
                pl.BlockSpec((TM_COMB * ROW_TILES, LANES), lambda i, z: (i, 0)),
            ],
            out_specs=pl.BlockSpec(memory_space=pl.ANY),
            scratch_shapes=[pltpu.VMEM((ESB * ROW_TILES, LANES), F32),
                            pltpu.SemaphoreType.DMA(()), pltpu.SemaphoreType.DMA(())],
        ),
        out_shape=jax.ShapeDtypeStruct((E_ROWS * ROW_TILES, LANES), F32),
        compiler_params=_cparams("arbitrary"),
        name="moe_gather",
    )(zrow, dest.reshape(n_blk, 1, N_COMB), h2_rows)


K_SPLIT = 4


def _cast_k_halves(w_refs):
    return jnp.concatenate([r[...].astype(BF16) for r in w_refs], axis=0)


def _expert1_kernel(te_ref, nv_ref, tb_ref, x_ref, *rest):
    wg_refs, wl_refs = rest[:K_SPLIT], rest[K_SPLIT:2 * K_SPLIT]
    bg_ref, bl_ref, h_ref, xb_s = rest[2 * K_SPLIT:]
    t = pl.program_id(0)
    j = pl.program_id(1)
    n = nv_ref[t]

    @pl.when(jnp.logical_and(j == 0, n > 0))
    def _to_matmul_layout():
        def to_matmul_layout(start, rows):
            for lt in range(ROW_TILES):
                v = x_ref[pl.ds(start * ROW_TILES + lt, rows, stride=ROW_TILES), :]
                xb_s[start:start + rows, lt * LANES:(lt + 1) * LANES] = v.astype(BF16)

        for start, rows in SUB_BLOCKS:
            pl.when(start < n)(functools.partial(to_matmul_layout, start, rows))

    @pl.when(n > 0)
    def _compute():
        wg = _cast_k_halves(wg_refs)
        wl = _cast_k_halves(wl_refs)

        def sub_block(start, rows):
            xs = xb_s[start:start + rows, :]
            glu = jnp.dot(xs, wg, preferred_element_type=F32) + bg_ref[...]
            lin = jnp.dot(xs, wl, preferred_element_type=F32) + bl_ref[...]
            glu = jnp.minimum(glu, SWIGLU_LIMIT)
            lin = jnp.clip(lin, -SWIGLU_LIMIT, SWIGLU_LIMIT)
            h_ref[start:start + rows, :] = (glu * jax.nn.sigmoid(SWIGLU_ALPHA * glu) * (lin + 1.0)).astype(BF16)

        sub_block(*SUB_BLOCKS[0])
        for start, rows in SUB_BLOCKS[1:]:
            pl.when(start < n)(functools.partial(sub_block, start, rows))


def _expert2_kernel(te_ref, nv_ref, tb_ref, h_ref, *rest):
    w_refs = rest[:K_SPLIT]
    b_ref, y_ref = rest[K_SPLIT:]
    t = pl.program_id(0)
    j = pl.program_id(1)
    n = nv_ref[t]

    @pl.when(n > 0)
    def _compute():
        w = _cast_k_halves(w_refs)

        def sub_block(start, rows):
            acc = jnp.dot(h_ref[start:start + rows, :], w, preferred_element_type=F32) + b_ref[...]
            for q in range(TN2 // LANES):
                first = start * ROW_TILES + j * (TN2 // LANES) + q
                y_ref[pl.ds(first, rows, stride=ROW_TILES), :] = acc[:, q * LANES:(q + 1) * LANES]

        sub_block(*SUB_BLOCKS[0])
        for start, rows in SUB_BLOCKS[1:]:
            pl.when(start < n)(functools.partial(sub_block, start, rows))


def _step(nv, t, j, last):
    return jnp.where(nv[t] > 0, j, last)


def _experts(layer, te, nv, tb, xs_rows, w1, b1, w2, b2):
    def w_spec(rows, cols, last, offset, part=0):
        return pl.BlockSpec((None, None, rows, cols),
                            lambda t, j, te, nv, tb: (layer, te[t], part, offset + _step(nv, t, j, last)))

    def w_parts(rows, cols, last, offset):
        return [w_spec(rows // K_SPLIT, cols, last, offset, part) for part in range(K_SPLIT)]

    b1r = b1.reshape(DEPTH, N_EXPERTS, 1, 2 * D_FF)
    h = pl.pallas_call(
        _expert1_kernel,
        grid_spec=pltpu.PrefetchScalarGridSpec(
            num_scalar_prefetch=3,
            grid=(N_ETILES, NJ1),
            in_specs=[
                pl.BlockSpec((ETM * ROW_TILES, LANES), lambda t, j, te, nv, tb: (tb[t], 0)),
                *w_parts(D_MODEL, TF, NJ1 - 1, 0), *w_parts(D_MODEL, TF, NJ1 - 1, NJ1),
                w_spec(1, TF, NJ1 - 1, 0), w_spec(1, TF, NJ1 - 1, NJ1),
            ],
            out_specs=pl.BlockSpec((ETM, TF), lambda t, j, te, nv, tb: (tb[t], _step(nv, t, j, NJ1 - 1))),
            scratch_shapes=[pltpu.VMEM((ETM, D_MODEL), BF16)],
        ),
        out_shape=jax.ShapeDtypeStruct((E_ROWS, D_FF), BF16),
        compiler_params=_cparams("arbitrary", "arbitrary"),
        name="moe_expert_in",
    )(te, nv, tb, xs_rows, *([w1] * (2 * K_SPLIT)), b1r, b1r)
    return pl.pallas_call(
        _expert2_kernel,
        grid_spec=pltpu.PrefetchScalarGridSpec(
            num_scalar_prefetch=3,
            grid=(N_ETILES, NJ2),
            in_specs=[
                pl.BlockSpec((ETM, D_FF), lambda t, j, te, nv, tb: (tb[t], 0)),
                *w_parts(D_FF, TN2, NJ2 - 1, 0), w_spec(1, TN2, NJ2 - 1, 0),
            ],
            out_specs=pl.BlockSpec((ETM * ROW_TILES, LANES), lambda t, j, te, nv, tb: (tb[t], 0)),
        ),
        out_shape=jax.ShapeDtypeStruct((E_ROWS * ROW_TILES, LANES), F32),
        compiler_params=_cparams("arbitrary", "arbitrary"),
        name="moe_expert_out",
    )(te, nv, tb, h, *([w2] * K_SPLIT), b2.reshape(DEPTH, N_EXPERTS, 1, D_MODEL))


def _combine_kernel(final, dcur_ref, dnext_ref, y_hbm, x_ref, gate_ref, gf_ref, gfin_ref, o_ref, buf, sem):
    i = pl.program_id(0)
    n = pl.num_programs(0)
    slot = i % 2

    def issue(d_ref, s):
        def body(tl, carry):
            for k in range(TOP_K):
                src = d_ref[0, 0, tl * TOP_K + k]
                pltpu.make_async_copy(
                    y_hbm.at[pl.ds(pl.multiple_of(src * ROW_TILES, ROW_TILES), ROW_TILES)],
                    buf.at[s, pl.ds(pl.multiple_of((k * TM_COMB + tl) * ROW_PITCH, SUBLANES), ROW_TILES)],
                    sem.at[s]).start()
            return carry
        lax.fori_loop(0, TM_COMB, body, 0, unroll=2)

    @pl.when(i == 0)
    def _():
        issue(dcur_ref, 0)

    @pl.when(i + 1 < n)
    def _():
        issue(dnext_ref, 1 - slot)

    landed = buf.at[slot, pl.ds(0, N_COMB * ROW_TILES)]
    pltpu.make_async_copy(landed, landed, sem.at[slot]).wait()

    gates = gate_ref[...]
    x = x_ref[...]
    gf = gf_ref[0]
    pieces = []
    for lt in range(ROW_TILES):
        acc = jnp.zeros((TM_COMB, LANES), F32)
        for k in range(TOP_K):
            v = buf[slot, pl.ds(k * TM_COMB * ROW_PITCH + lt, TM_COMB, stride=ROW_PITCH), :]
            acc = acc + gates[:, k:k + 1] * v
        lc = slice(lt * LANES, (lt + 1) * LANES)
        pieces.append(x[:, lc] + gf[:, lc] * acc)
    out = jnp.concatenate(pieces, axis=-1)
    if final:
        out = _rms(out) * gfin_ref[...]
    o_ref[...] = out


def _combine(dest, y_rows, x_new, gates, mod, g_final, final):
    n_blk = N_TOK // TM_COMB
    p_tiles = N_P // TM_COMB
    d3 = dest.reshape(n_blk, 1, N_COMB)
    return pl.pallas_call(
        functools.partial(_combine_kernel, final),
        grid=(n_blk,),
        in_specs=[
            pl.BlockSpec((1, 1, N_COMB), lambda i: (i, 0, 0), memory_space=pltpu.SMEM),
            pl.BlockSpec((1, 1, N_COMB), lambda i: (jnp.minimum(i + 1, n_blk - 1), 0, 0), memory_space=pltpu.SMEM),
            pl.BlockSpec(memory_space=pl.ANY),
            pl.BlockSpec((TM_COMB, D_MODEL), lambda i: (i, 0)),
            pl.BlockSpec((TM_COMB, LANES), lambda i: (i, 0)),
            pl.BlockSpec((1, MOD_ROWS, D_MODEL), lambda i: (i // p_tiles, 0, 5)),
            _const_spec((1, D_MODEL)),
        ],
        out_specs=pl.BlockSpec((TM_COMB, D_MODEL), lambda i: (i, 0)),
        out_shape=jax.ShapeDtypeStruct((N_TOK, D_MODEL), F32),
        scratch_shapes=[pltpu.VMEM((2, N_COMB * ROW_PITCH, LANES), F32), pltpu.SemaphoreType.DMA((2,))],
        compiler_params=_cparams("arbitrary"),
        name="moe_combine",
    )(d3, d3, y_rows, x_new, gates, mod, g_final.reshape(1, D_MODEL))


def kernel(x_prompt, x_sample, c_prompt, c_sample, state_ssm_re, state_ssm_im, g_mix, w_ada, b_ada, w_in, lam_re, lam_im, log_dt, b_re, b_im, c_re, c_im, d_skip, w_glu, b_glu, ln_v_g, ln_v_b, w_s, b_s, g_out_a, g_out_b, w_out, g_ffn, w_router, b_router, w1, b1, w2, b2, g_final):
    x = jnp.concatenate([
        x_prompt.transpose(1, 0, 2).reshape(N_P, D_MODEL),
        x_sample.transpose(1, 0, 2).reshape(N_S, D_MODEL)], axis=0)

    c_all = jnp.concatenate([c_prompt, jnp.zeros((8 - BATCH, D_MODEL), F32), c_sample], axis=0)
    mod_all = _ada_mod(c_all, w_ada, b_ada)
    mod_tok = jnp.stack([jnp.tile(mod_all[:, :BATCH], (1, MOD_ROWS // BATCH, 1)), mod_all[:, 8:]], axis=1)

    ab_re, ab_im, bb_re, bb_im = _discretise(lam_re, lam_im, log_dt, b_re, b_im)
    bcat, ccat, abr, abi = _block_diag_params(ab_re, ab_im, bb_re, bb_im, c_re, c_im)

    r_idx = jnp.arange(TM_MIX, dtype=I32)
    spread = (r_idx[:, None] // BATCH == jnp.arange(CHUNK, dtype=I32)[None, :]).astype(F32)
    wk = jnp.einsum("rt,lhts,cs->lhrc", spread, w_s, spread, precision=lax.Precision.HIGHEST)
    wk = jnp.where(r_idx[:, None] % BATCH == r_idx[None, :] % BATCH, wk, 0.0).astype(BF16)
    bsr = jnp.repeat(b_s.transpose(0, 2, 1), BATCH, axis=1)
    ws4 = jnp.repeat(w_s[:, :, :DEC_SEQ, :DEC_SEQ].transpose(0, 2, 3, 1).reshape(
        DEPTH, DEC_SEQ * DEC_SEQ, N_GMLP_HEADS), GMLP_HEAD, axis=2)
    bs4 = jnp.repeat(b_s[:, :, :DEC_SEQ].transpose(0, 2, 1), GMLP_HEAD, axis=2)
    wr_pad = jnp.pad(w_router, ((0, 0), (0, 0), (0, LANES - N_EXPERTS)))
    br_pad = jnp.pad(b_router, ((0, 0), (0, LANES - N_EXPERTS)), constant_values=-1e30).reshape(DEPTH, 1, LANES)

    h0_re = state_ssm_re.reshape(DEPTH, DEC_BATCH, N_SSM_GROUPS * SSM_STATE)
    h0_im = state_ssm_im.reshape(DEPTH, DEC_BATCH, N_SSM_GROUPS * SSM_STATE)

    st_p_re, st_p_im, st_s_re, st_s_im, v_rows = [], [], [], [], []
    for l in range(DEPTH):
        mod = mod_tok[l]
        u_a, gu_b, vn = _mix_in(x, mod, g_mix[l], w_in[l].astype(BF16), ln_v_g[l], ln_v_b[l])
        wglu_bf = w_glu[l].astype(BF16)
        ya_p, hp_re, hp_im = _s5_prompt(u_a, bcat[l], ccat[l], abr[l], abi[l], d_skip[l], wglu_bf, b_glu[l])
        ya_s, hs_re, hs_im = _s5_sample(u_a, h0_re[l], h0_im[l], bcat[l], ccat[l], abr[l], abi[l], d_skip[l],
                                        wglu_bf, b_glu[l])
        y_a = jnp.concatenate([ya_p, ya_s], axis=0)
        y_b = _gmlp(gu_b, vn, wk[l], bsr[l], ws4[l], bs4[l])
        x_new, h2_rows, idx, gates = _mix_out(
            x, y_a, y_b, g_out_a[l], g_out_b[l], w_out[l].astype(BF16), mod, g_ffn[l], wr_pad[l], br_pad[l])
        dest, te, nv, tb, zrow = _route(idx[:, :TOP_K])
        xs_rows = _gather_rows(dest, zrow, h2_rows)
        y_rows = _experts(l, te, nv, tb, xs_rows, w1, b1, w2, b2)
        x = _combine(dest, y_rows, x_new, gates, mod, g_final, l == DEPTH - 1)

        unblock = lambda h: h[:, :BATCH].transpose(1, 0, 2).reshape(BATCH, N_SSM_GROUPS, SSM_STATE)
        st_p_re.append(unblock(hp_re))
        st_p_im.append(unblock(hp_im))
        st_s_re.append(hs_re.reshape(DEC_BATCH, N_SSM_GROUPS, SSM_STATE))
        st_s_im.append(hs_im.reshape(DEC_BATCH, N_SSM_GROUPS, SSM_STATE))
        v_rows.append(vn[N_P:].reshape(DEC_SEQ, DEC_BATCH, D_GMLP).transpose(1, 0, 2))

    y_prompt = x[:N_P].reshape(SEQ, BATCH, D_MODEL).transpose(1, 0, 2)
    y_sample = x[N_P:].reshape(DEC_SEQ, DEC_BATCH, D_MODEL).transpose(1, 0, 2)
    return (y_prompt, y_sample, jnp.stack(st_p_re), jnp.stack(st_p_im),
            jnp.stack(st_s_re), jnp.stack(st_s_im), jnp.stack(v_rows))
```

```python
import functools

import jax
import jax.numpy as jnp
from jax import lax
from jax.experimental import pallas as pl
from jax.experimental.pallas import tpu as pltpu

F32 = jnp.float32
BF16 = jnp.bfloat16
I32 = jnp.int32

D_MODEL = 2048
BATCH = 4
SEQ = 2048
DEPTH = 4
DEC_BATCH = 128
DEC_SEQ = 4
D_SSM = 1024
SSM_GROUP = 16
N_SSM_GROUPS = 64
SSM_STATE = 64
D_GMLP = 1024
GMLP_HEAD = 128
N_GMLP_HEADS = 8
CHUNK = 128
D_IN = D_SSM + 2 * D_GMLP
N_EXPERTS = 32
TOP_K = 4
D_FF = D_MODEL
SWIGLU_ALPHA = 1.702
SWIGLU_LIMIT = 7.0
N_MOD = 6
EPS = 1e-5

LANES = 128
SUBLANES = 8
N_P = BATCH * SEQ
N_S = DEC_BATCH * DEC_SEQ
N_TOK = N_P + N_S
N_ASSIGN = N_TOK * TOP_K
C_ROWS = 8 + DEC_BATCH
MOD_ROWS = 128

TM = 256
TM_S5 = 512
TM_MIX = CHUNK * BATCH
TM_COMB = 128
ROW_TILES = D_MODEL // LANES
ROW_PITCH = ROW_TILES + SUBLANES

GROUP_BLOCK = 16
N_GB = N_SSM_GROUPS // GROUP_BLOCK
GB_CH = GROUP_BLOCK * SSM_GROUP
GB_ST = GROUP_BLOCK * SSM_STATE
SCAN_LANES = 512

ESB = 256
N_ESB = 5
ETM = N_ESB * ESB
ESB_TAIL = 128
SUB_BLOCKS = ([(i * ESB, ESB) for i in range(N_ESB - 1)]
              + [(ETM - ESB + i * ESB_TAIL, ESB_TAIL) for i in range(ESB // ESB_TAIL)])
N_ETILES = N_ASSIGN // ETM + N_EXPERTS
E_ROWS = N_ETILES * ETM
TF = 512
NJ1 = D_FF // TF
TN2 = 512
NJ2 = D_MODEL // TN2
N_COMB = TM_COMB * TOP_K

VMEM_LIMIT = 56 * 1024 * 1024


def _cparams(*sem):
    return pltpu.CompilerParams(dimension_semantics=sem, vmem_limit_bytes=VMEM_LIMIT)


def _const_spec(shape):
    zeros = (0,) * len(shape)
    return pl.BlockSpec(shape, lambda *_: zeros, pipeline_mode=pl.Buffered(1))


def _rms(x):
    return x * lax.rsqrt(jnp.mean(x * x, axis=-1, keepdims=True) + EPS)


def _modulate(xn, sc_ref, sh_ref):
    rows, dim = xn.shape
    x3 = xn.reshape(rows // MOD_ROWS, MOD_ROWS, dim)
    return (x3 * (1.0 + sc_ref[0][None]) + sh_ref[0][None]).reshape(rows, dim)


def _gate(v, g_ref):
    rows, dim = v.shape
    return (v.reshape(rows // MOD_ROWS, MOD_ROWS, dim) * g_ref[0][None]).reshape(rows, dim)


def _ada_kernel(c_ref, w_ref, b_ref, o_ref):
    s = jax.nn.silu(c_ref[...]).astype(BF16)
    o_ref[0] = jnp.dot(s, w_ref[0].astype(BF16), preferred_element_type=F32) + b_ref[0]


def _ada_mod(c_all, w_ada, b_ada):
    tn = 1024
    nj = N_MOD * D_MODEL // tn
    return pl.pallas_call(
        _ada_kernel,
        grid=(DEPTH, nj),
        in_specs=[
            pl.BlockSpec((C_ROWS, D_MODEL), lambda l, j: (0, 0)),
            pl.BlockSpec((1, D_MODEL, tn), lambda l, j: (l, 0, j)),
            pl.BlockSpec((1, 1, tn), lambda l, j: (l, 0, j)),
        ],
        out_specs=pl.BlockSpec((1, C_ROWS, tn), lambda l, j: (l, 0, j)),
        out_shape=jax.ShapeDtypeStruct((DEPTH, C_ROWS, N_MOD * D_MODEL), F32),
        compiler_params=_cparams("arbitrary", "arbitrary"),
        name="ada_mod",
    )(c_all, w_ada, b_ada.reshape(DEPTH, 1, N_MOD * D_MODEL))


def _disc_kernel(lr_ref, li_ref, ldt_ref, br_ref, bi_ref, abr_ref, abi_ref, bbr_ref, bbi_ref):
    lr = lr_ref[...]
    li = li_ref[...]
    dt = jnp.exp(ldt_ref[...])
    mag = jnp.exp(lr * dt)
    ab_re = mag * jnp.cos(li * dt)
    ab_im = mag * jnp.sin(li * dt)
    den = lr * lr + li * li
    q_re = ((ab_re - 1.0) * lr + ab_im * li) / den
    q_im = (ab_im * lr - (ab_re - 1.0) * li) / den
    br = br_ref[...]
    bi = bi_ref[...]
    abr_ref[...] = ab_re
    abi_ref[...] = ab_im
    bbr_ref[...] = q_re * br - q_im * bi
    bbi_ref[...] = q_re * bi + q_im * br


def _discretise(lam_re, lam_im, log_dt, b_re, b_im):
    rows = DEPTH * N_SSM_GROUPS
    cols = SSM_STATE * SSM_GROUP
    rep = lambda a: jnp.repeat(a.reshape(rows, SSM_STATE), SSM_GROUP, axis=1)
    full = pl.BlockSpec((rows, cols), lambda: (0, 0))
    out = jax.ShapeDtypeStruct((rows, cols), F32)
    return pl.pallas_call(
        _disc_kernel,
        in_specs=[full, full, pl.BlockSpec((rows, 1), lambda: (0, 0)), full, full],
        out_specs=[full] * 4,
        out_shape=[out] * 4,
        name="s5_discretise",
    )(rep(lam_re), rep(lam_im), log_dt.reshape(rows, 1),
      b_re.reshape(rows, cols), b_im.reshape(rows, cols))


def _block_diag_params(ab_re, ab_im, bb_re, bb_im, c_re, c_im):
    eye = jnp.eye(GROUP_BLOCK, dtype=F32)
    shp = (DEPTH, N_GB, GROUP_BLOCK, SSM_STATE, SSM_GROUP)

    def in_mat(bb):
        m = bb.reshape(shp)[:, :, :, None, :, :] * eye[None, None, :, :, None, None]
        return m.transpose(0, 1, 2, 5, 3, 4).reshape(DEPTH, N_GB, GB_CH, GB_ST)

    def out_mat(cc):
        c5 = cc.reshape(DEPTH, N_GB, GROUP_BLOCK, SSM_GROUP, SSM_STATE)
        m = c5[:, :, :, None, :, :] * eye[None, None, :, :, None, None]
        return m.transpose(0, 1, 2, 5, 3, 4).reshape(DEPTH, N_GB, GB_ST, GB_CH)

    bcat = jnp.concatenate([in_mat(bb_re), in_mat(bb_im)], axis=-1).astype(BF16)
    ccat = jnp.concatenate([out_mat(c_re), -out_mat(c_im)], axis=-2).astype(BF16)
    take = lambda a: a.reshape(DEPTH, N_GB, GROUP_BLOCK, SSM_STATE, SSM_GROUP)[..., 0].reshape(DEPTH, N_GB, 1, GB_ST)
    return bcat, ccat, take(ab_re), take(ab_im)


def _mix_in_kernel(x_ref, sh_ref, sc_ref, g_ref, w_ref, lng_ref, lnb_ref, ua_ref, gub_ref, vn_ref):
    xn = _rms(x_ref[...]) * g_ref[...]
    h = _modulate(xn, sc_ref, sh_ref)
    proj = jnp.dot(h.astype(BF16), w_ref[...], preferred_element_type=F32)
    ua_ref[...] = proj[:, :D_SSM]
    gub_ref[...] = jax.nn.gelu(proj[:, D_SSM:D_SSM + D_GMLP])
    gv = jax.nn.gelu(proj[:, D_SSM + D_GMLP:])
    mu = jnp.mean(gv, axis=-1, keepdims=True)
    var = jnp.mean(jnp.square(gv - mu), axis=-1, keepdims=True)
    vn_ref[...] = (gv - mu) * lax.rsqrt(var + EPS) * lng_ref[...] + lnb_ref[...]


def _mix_in(x, mod, g_mix, w_in_bf, ln_g, ln_b):
    p_tiles = N_P // TM
    mod_spec = lambda m: pl.BlockSpec((1, MOD_ROWS, D_MODEL), lambda i: (i // p_tiles, 0, m))
    out = jax.ShapeDtypeStruct((N_TOK, D_SSM), F32)
    return pl.pallas_call(
        _mix_in_kernel,
        grid=(N_TOK // TM,),
        in_specs=[
            pl.BlockSpec((TM, D_MODEL), lambda i: (i, 0)),
            mod_spec(0), mod_spec(1),
            _const_spec((1, D_MODEL)),
            _const_spec((D_MODEL, D_IN)),
            _const_spec((1, D_GMLP)), _const_spec((1, D_GMLP)),
        ],
        out_specs=[pl.BlockSpec((TM, D_SSM), lambda i: (i, 0))] * 3,
        out_shape=[out] * 3,
        compiler_params=_cparams("arbitrary"),
        name="mix_in",
    )(x, mod, mod, g_mix.reshape(1, D_MODEL), w_in_bf, ln_g.reshape(1, D_GMLP), ln_b.reshape(1, D_GMLP))


def _cmul_add(ar, ai, hr, hi, vr, vi):
    return ar * hr - ai * hi + vr, ar * hi + ai * hr + vi


def _s5_glu(y_s, wglu_ref, bglu_ref, ya_ref):
    ya = jax.nn.gelu(y_s[...])
    z = jnp.dot(ya.astype(BF16), wglu_ref[...], preferred_element_type=F32) + bglu_ref[...]
    ya_ref[...] = ya * jax.nn.sigmoid(z)


def _s5_prompt_kernel(u_ref, bcat_ref, ccat_ref, abr_ref, abi_ref, dsk_ref, wglu_ref, bglu_ref,
                      ya_ref, hre_ref, him_ref, bu_s, hs_s, y_s):
    @pl.when(pl.program_id(0) == 0)
    def _():
        hre_ref[...] = jnp.zeros_like(hre_ref)
        him_ref[...] = jnp.zeros_like(him_ref)

    u = u_ref[...]
    ub = u.astype(BF16)
    first_step = lax.broadcasted_iota(I32, (SUBLANES, SCAN_LANES), 0) < BATCH
    for gb in range(N_GB):
        cols = slice(gb * GB_CH, (gb + 1) * GB_CH)
        bu_s[...] = jnp.dot(ub[:, cols], bcat_ref[gb], preferred_element_type=F32)
        for lh in range(GB_ST // SCAN_LANES):
            re_c = slice(lh * SCAN_LANES, (lh + 1) * SCAN_LANES)
            im_c = slice(GB_ST + lh * SCAN_LANES, GB_ST + (lh + 1) * SCAN_LANES)
            ar = jnp.broadcast_to(abr_ref[gb, :, re_c], (SUBLANES, SCAN_LANES))
            ai = jnp.broadcast_to(abi_ref[gb, :, re_c], (SUBLANES, SCAN_LANES))

            def body(k, carry):
                hr, hi = carry
                rows = pl.ds(pl.multiple_of(k * SUBLANES, SUBLANES), SUBLANES)
                vr = bu_s[rows, re_c]
                vi = bu_s[rows, im_c]
                h1r, h1i = _cmul_add(ar, ai, hr, hi, vr, vi)
                h2r, h2i = _cmul_add(ar, ai, pltpu.roll(h1r, BATCH, 0), pltpu.roll(h1i, BATCH, 0), vr, vi)
                hs_s[rows, re_c] = jnp.where(first_step, h1r, h2r)
                hs_s[rows, im_c] = jnp.where(first_step, h1i, h2i)
                return pltpu.roll(h2r, BATCH, 0), pltpu.roll(h2i, BATCH, 0)

            hr, hi = lax.fori_loop(0, TM_S5 // SUBLANES, body, (hre_ref[gb, :, re_c], him_ref[gb, :, re_c]))
            hre_ref[gb, :, re_c] = hr
            him_ref[gb, :, re_c] = hi
        y = jnp.dot(hs_s[...].astype(BF16), ccat_ref[gb], preferred_element_type=F32)
        y_s[:, cols] = y + dsk_ref[:, cols] * u[:, cols]
    _s5_glu(y_s, wglu_ref, bglu_ref, ya_ref)


def _s5_sample_kernel(u_ref, h0r_ref, h0i_ref, bcat_ref, ccat_ref, abr_ref, abi_ref, dsk_ref, wglu_ref,
                      bglu_ref, ya_ref, hre_ref, him_ref, bu_s, hs_s, y_s):
    u = u_ref[...]
    ub = u.astype(BF16)
    for gb in range(N_GB):
        cols = slice(gb * GB_CH, (gb + 1) * GB_CH)
        bu_s[...] = jnp.dot(ub[:, cols], bcat_ref[gb], preferred_element_type=F32)
        for lh in range(GB_ST // SCAN_LANES):
            re_c = slice(lh * SCAN_LANES, (lh + 1) * SCAN_LANES)
            im_c = slice(GB_ST + lh * SCAN_LANES, GB_ST + (lh + 1) * SCAN_LANES)
            st_c = slice(gb * GB_ST + lh * SCAN_LANES, gb * GB_ST + (lh + 1) * SCAN_LANES)
            ar = jnp.broadcast_to(abr_ref[gb, :, re_c], (SUBLANES, SCAN_LANES))
            ai = jnp.broadcast_to(abi_ref[gb, :, re_c], (SUBLANES, SCAN_LANES))

            def body(q, carry):
                r0 = pl.multiple_of(q * SUBLANES, SUBLANES)
                hr = h0r_ref[pl.ds(r0, SUBLANES), st_c]
                hi = h0i_ref[pl.ds(r0, SUBLANES), st_c]
                for t in range(DEC_SEQ):
                    rows = pl.ds(t * DEC_BATCH + r0, SUBLANES)
                    hr, hi = _cmul_add(ar, ai, hr, hi, bu_s[rows, re_c], bu_s[rows, im_c])
                    hs_s[rows, re_c] = hr
                    hs_s[rows, im_c] = hi
                hre_ref[pl.ds(r0, SUBLANES), st_c] = hr
                him_ref[pl.ds(r0, SUBLANES), st_c] = hi
                return carry

            lax.fori_loop(0, DEC_BATCH // SUBLANES, body, 0)
        y = jnp.dot(hs_s[...].astype(BF16), ccat_ref[gb], preferred_element_type=F32)
        y_s[:, cols] = y + dsk_ref[:, cols] * u[:, cols]
    _s5_glu(y_s, wglu_ref, bglu_ref, ya_ref)


def _s5_common_specs():
    return [
        _const_spec((N_GB, GB_CH, 2 * GB_ST)),
        _const_spec((N_GB, 2 * GB_ST, GB_CH)),
        _const_spec((N_GB, 1, GB_ST)), _const_spec((N_GB, 1, GB_ST)),
        _const_spec((1, D_SSM)),
        _const_spec((D_SSM, D_SSM)),
        _const_spec((1, D_SSM)),
    ]


def _s5_prompt(u_a, bcat, ccat, abr, abi, d_skip, w_glu_bf, b_glu):
    st = jax.ShapeDtypeStruct((N_GB, SUBLANES, GB_ST), F32)
    st_spec = pl.BlockSpec((N_GB, SUBLANES, GB_ST), lambda i: (0, 0, 0))
    return pl.pallas_call(
        _s5_prompt_kernel,
        grid=(N_P // TM_S5,),
        in_specs=[pl.BlockSpec((TM_S5, D_SSM), lambda i: (i, 0))] + _s5_common_specs(),
        out_specs=[pl.BlockSpec((TM_S5, D_SSM), lambda i: (i, 0)), st_spec, st_spec],
        out_shape=[jax.ShapeDtypeStruct((N_P, D_SSM), F32), st, st],
        scratch_shapes=[pltpu.VMEM((TM_S5, 2 * GB_ST), F32), pltpu.VMEM((TM_S5, 2 * GB_ST), F32),
                        pltpu.VMEM((TM_S5, D_SSM), F32)],
        compiler_params=_cparams("arbitrary"),
        name="s5_prompt",
    )(u_a, bcat, ccat, abr, abi, d_skip.reshape(1, D_SSM), w_glu_bf, b_glu.reshape(1, D_SSM))


def _s5_sample(u_a, h0_re, h0_im, bcat, ccat, abr, abi, d_skip, w_glu_bf, b_glu):
    st = jax.ShapeDtypeStruct((DEC_BATCH, N_SSM_GROUPS * SSM_STATE), F32)
    st_spec = pl.BlockSpec((DEC_BATCH, N_SSM_GROUPS * SSM_STATE), lambda i: (0, 0))
    return pl.pallas_call(
        _s5_sample_kernel,
        grid=(1,),
        in_specs=[pl.BlockSpec((N_S, D_SSM), lambda i: (N_P // N_S, 0)), st_spec, st_spec] + _s5_common_specs(),
        out_specs=[pl.BlockSpec((N_S, D_SSM), lambda i: (0, 0)), st_spec, st_spec],
        out_shape=[jax.ShapeDtypeStruct((N_S, D_SSM), F32), st, st],
        scratch_shapes=[pltpu.VMEM((N_S, 2 * GB_ST), F32), pltpu.VMEM((N_S, 2 * GB_ST), F32),
                        pltpu.VMEM((N_S, D_SSM), F32)],
        compiler_params=_cparams("arbitrary"),
        name="s5_sample",
    )(u_a, h0_re, h0_im, bcat, ccat, abr, abi, d_skip.reshape(1, D_SSM), w_glu_bf, b_glu.reshape(1, D_SSM))


def _gmlp_kernel(gub_ref, vn_ref, wk_ref, bsr_ref, ws4_ref, bs4_ref, yb_ref):
    i = pl.program_id(0)

    @pl.when(i < N_P // TM_MIX)
    def _prompt_chunk():
        vb = vn_ref[...].astype(BF16)
        row = lax.broadcasted_iota(I32, (TM_MIX, TM_MIX), 0)
        col = lax.broadcasted_iota(I32, (TM_MIX, TM_MIX), 1)
        for h in range(N_GMLP_HEADS):
            hc = slice(h * GMLP_HEAD, (h + 1) * GMLP_HEAD)
            w = jnp.where(row >= col, wk_ref[h], jnp.zeros((), BF16))
            mix = jnp.dot(w, vb[:, hc], preferred_element_type=F32) + bsr_ref[:, h:h + 1]
            yb_ref[:, hc] = gub_ref[:, hc] * mix

    @pl.when(i == N_P // TM_MIX)
    def _sample_chunk():
        for t in range(DEC_SEQ):
            acc = jnp.broadcast_to(bs4_ref[t:t + 1, :], (DEC_BATCH, D_GMLP))
            for s in range(t + 1):
                w_ts = ws4_ref[t * DEC_SEQ + s:t * DEC_SEQ + s + 1, :]
                acc = acc + w_ts * vn_ref[s * DEC_BATCH:(s + 1) * DEC_BATCH, :]
            rows = slice(t * DEC_BATCH, (t + 1) * DEC_BATCH)
            yb_ref[rows, :] = gub_ref[rows, :] * acc


def _gmlp(gu_b, vn, wk, bsr, ws4, bs4):
    tok = pl.BlockSpec((TM_MIX, D_GMLP), lambda i: (i, 0))
    return pl.pallas_call(
        _gmlp_kernel,
        grid=(N_TOK // TM_MIX,),
        in_specs=[
            tok, tok,
            _const_spec((N_GMLP_HEADS, TM_MIX, TM_MIX)),
            _const_spec((TM_MIX, N_GMLP_HEADS)),
            _const_spec((DEC_SEQ * DEC_SEQ, D_GMLP)),
            _const_spec((DEC_SEQ, D_GMLP)),
        ],
        out_specs=tok,
        out_shape=jax.ShapeDtypeStruct((N_TOK, D_GMLP), F32),
        compiler_params=_cparams("arbitrary"),
        name="gmlp_mix",
    )(gu_b, vn, wk, bsr, ws4, bs4)


def _mix_out_kernel(x_ref, ya_ref, yb_ref, goa_ref, gob_ref, wout_ref, gm_ref, shf_ref, scf_ref, gffn_ref,
                    wr_ref, br_ref, xnew_ref, h2_ref, idx_ref, gate_ref):
    na = (_rms(ya_ref[...]) * goa_ref[...]).astype(BF16)
    nb = (_rms(yb_ref[...]) * gob_ref[...]).astype(BF16)
    o = (jnp.dot(na, wout_ref[:D_SSM, :], preferred_element_type=F32)
         + jnp.dot(nb, wout_ref[D_SSM:, :], preferred_element_type=F32))
    x_new = x_ref[...] + _gate(o, gm_ref)
    xnew_ref[...] = x_new

    h2 = _modulate(_rms(x_new) * gffn_ref[...], scf_ref, shf_ref)
    for lt in range(ROW_TILES):
        h2_ref[pl.ds(lt, TM, stride=ROW_TILES), :] = h2[:, lt * LANES:(lt + 1) * LANES]

    logits = jnp.dot(h2, wr_ref[...], preferred_element_type=F32, precision=lax.Precision.HIGHEST) + br_ref[...]
    lane = lax.broadcasted_iota(I32, (TM, LANES), 1)
    work = logits
    idx_out = jnp.zeros((TM, LANES), I32)
    exp_out = jnp.zeros((TM, LANES), F32)
    denom = jnp.zeros((TM, 1), F32)
    top = None
    for k in range(TOP_K):
        m = jnp.max(work, axis=-1, keepdims=True)
        sel = jnp.min(jnp.where(work == m, lane, LANES), axis=-1, keepdims=True)
        top = m if k == 0 else top
        e = jnp.exp(m - top)
        denom = denom + e
        idx_out = jnp.where(lane == k, sel, idx_out)
        exp_out = jnp.where(lane == k, e, exp_out)
        work = jnp.where(lane == sel, -jnp.inf, work)
    idx_ref[...] = idx_out
    gate_ref[...] = exp_out / denom


def _mix_out(x, y_a, y_b, g_out_a, g_out_b, w_out_bf, mod, g_ffn, wr_pad, br_pad):
    p_tiles = N_P // TM
    tok = lambda w: pl.BlockSpec((TM, w), lambda i: (i, 0))
    mod_spec = lambda m: pl.BlockSpec((1, MOD_ROWS, D_MODEL), lambda i: (i // p_tiles, 0, m))
    return pl.pallas_call(
        _mix_out_kernel,
        grid=(N_TOK // TM,),
        in_specs=[
            tok(D_MODEL), tok(D_SSM), tok(D_GMLP),
            _const_spec((1, D_SSM)), _const_spec((1, D_GMLP)),
            _const_spec((D_MODEL, D_MODEL)),
            mod_spec(2), mod_spec(3), mod_spec(4),
            _const_spec((1, D_MODEL)),
            _const_spec((D_MODEL, LANES)), _const_spec((1, LANES)),
        ],
        out_specs=[
            tok(D_MODEL),
            pl.BlockSpec((TM * ROW_TILES, LANES), lambda i: (i, 0)),
            tok(LANES), tok(LANES),
        ],
        out_shape=[
            jax.ShapeDtypeStruct((N_TOK, D_MODEL), F32),
            jax.ShapeDtypeStruct((N_TOK * ROW_TILES, LANES), F32),
            jax.ShapeDtypeStruct((N_TOK, LANES), I32),
            jax.ShapeDtypeStruct((N_TOK, LANES), F32),
        ],
        compiler_params=_cparams("arbitrary"),
        name="mix_out_router",
    )(x, y_a, y_b, g_out_a.reshape(1, D_SSM), g_out_b.reshape(1, D_GMLP),
      w_out_bf, mod, mod, mod, g_ffn.reshape(1, D_MODEL), wr_pad, br_pad)


def _route(idx):
    flat_e = idx.reshape(-1)
    onehot = (flat_e[:, None] == jnp.arange(N_EXPERTS, dtype=I32)[None, :]).astype(I32)
    csum = jnp.cumsum(onehot, axis=0)
    rank = jnp.take_along_axis(csum, flat_e[:, None], axis=1)[:, 0] - 1
    counts = csum[-1]
    ntiles = (counts + ETM - 1) // ETM
    tile_end = jnp.cumsum(ntiles)
    tile0 = tile_end - ntiles
    dest = tile0[flat_e] * ETM + rank
    n_used = tile_end[-1]
    t = jnp.arange(N_ETILES, dtype=I32)
    tb = jnp.minimum(t, n_used - 1)
    te = jnp.minimum(jnp.sum((tile_end[None, :] <= tb[:, None]).astype(I32), axis=1), N_EXPERTS - 1)
    nv = jnp.where(t < n_used, jnp.clip(counts[te] - (t - tile0[te]) * ETM, 0, ETM), 0).astype(I32)
    full = counts // ESB * ESB
    zrow = jnp.where(counts % ESB != 0, tile0 * ETM + full, -1).astype(I32)
    return dest.astype(I32), te, nv, tb, zrow


def _row_start(row):
    return row * ROW_TILES if isinstance(row, int) else pl.multiple_of(row * ROW_TILES, ROW_TILES)


def _row_copy(src_hbm, src_row, dst_hbm, dst_row, n_rows, sem):
    return pltpu.make_async_copy(
        src_hbm.at[pl.ds(_row_start(src_row), n_rows * ROW_TILES)],
        dst_hbm.at[pl.ds(_row_start(dst_row), n_rows * ROW_TILES)],
        sem)


def _gather_kernel(zrow_ref, dest_ref, h2_ref, xs_hbm, zeros_v, sem_z, sem):
    @pl.when(pl.program_id(0) == 0)
    def _zero_partial_subblocks():
        zeros_v[...] = jnp.zeros_like(zeros_v)
        for e in range(N_EXPERTS):
            @pl.when(zrow_ref[e] >= 0)
            def _():
                _row_copy(zeros_v, 0, xs_hbm, zrow_ref[e], ESB, sem_z).start()
        for e in range(N_EXPERTS):
            @pl.when(zrow_ref[e] >= 0)
            def _():
                _row_copy(zeros_v, 0, xs_hbm, zrow_ref[e], ESB, sem_z).wait()

    def body(tl, carry):
        for k in range(TOP_K):
            _row_copy(h2_ref, tl, xs_hbm, dest_ref[0, 0, tl * TOP_K + k], 1, sem).start(priority=k % 2)
        return carry

    lax.fori_loop(0, TM_COMB, body, 0, unroll=2)
    for _ in range(TOP_K):
        _row_copy(h2_ref, 0, xs_hbm, 0, TM_COMB, sem).wait()


def _gather_rows(dest, zrow, h2_rows):
    n_blk = N_TOK // TM_COMB
    return pl.pallas_call(
        _gather_kernel,
        grid_spec=pltpu.PrefetchScalarGridSpec(
            num_scalar_prefetch=1,
            grid=(n_blk,),
            in_specs=[
                pl.BlockSpec((1, 1, N_COMB), lambda i, z: (i, 0, 0), memory_space=pltpu.SMEM),
                pl.BlockSpec((TM_COMB * ROW_TILES, LANES), lambda i, z: (i, 0)),
            ],
            out_specs=pl.BlockSpec(memory_space=pl.ANY),
            scratch_shapes=[pltpu.VMEM((ESB * ROW_TILES, LANES), F32),
                            pltpu.SemaphoreType.DMA(()), pltpu.SemaphoreType.DMA(())],
        ),
        out_shape=jax.ShapeDtypeStruct((E_ROWS * ROW_TILES, LANES), F32),
        compiler_params=_cparams("arbitrary"),
        name="moe_gather",
    )(zrow, dest.reshape(n_blk, 1, N_COMB), h2_rows)


def _expert1_kernel(te_ref, nv_ref, tb_ref, x_ref, wg_ref, wl_ref, bg_ref, bl_ref, h_ref, xb_s):
    t = pl.program_id(0)
    j = pl.program_id(1)
    n = nv_ref[t]

    @pl.when(jnp.logical_and(j == 0, n > 0))
    def _to_matmul_layout():
        def to_matmul_layout(start, rows):
            for lt in range(ROW_TILES):
                v = x_ref[pl.ds(start * ROW_TILES + lt, rows, stride=ROW_TILES), :]
                xb_s[start:start + rows, lt * LANES:(lt + 1) * LANES] = v.astype(BF16)

        for start, rows in SUB_BLOCKS:
            pl.when(start < n)(functools.partial(to_matmul_layout, start, rows))

    @pl.when(n > 0)
    def _compute():
        wg = wg_ref[...].astype(BF16)
        wl = wl_ref[...].astype(BF16)

        def sub_block(start, rows):
            xs = xb_s[start:start + rows, :]
            glu = jnp.dot(xs, wg, preferred_element_type=F32) + bg_ref[...]
            lin = jnp.dot(xs, wl, preferred_element_type=F32) + bl_ref[...]
            glu = jnp.minimum(glu, SWIGLU_LIMIT)
            lin = jnp.clip(lin, -SWIGLU_LIMIT, SWIGLU_LIMIT)
            h_ref[start:start + rows, :] = (glu * jax.nn.sigmoid(SWIGLU_ALPHA * glu) * (lin + 1.0)).astype(BF16)

        sub_block(*SUB_BLOCKS[0])
        for start, rows in SUB_BLOCKS[1:]:
            pl.when(start < n)(functools.partial(sub_block, start, rows))


def _expert2_kernel(te_ref, nv_ref, tb_ref, h_ref, w_ref, b_ref, y_ref):
    t = pl.program_id(0)
    j = pl.program_id(1)
    n = nv_ref[t]

    @pl.when(n > 0)
    def _compute():
        w = w_ref[...].astype(BF16)

        def sub_block(start, rows):
            acc = jnp.dot(h_ref[start:start + rows, :], w, preferred_element_type=F32) + b_ref[...]
            for q in range(TN2 // LANES):
                first = start * ROW_TILES + j * (TN2 // LANES) + q
                y_ref[pl.ds(first, rows, stride=ROW_TILES), :] = acc[:, q * LANES:(q + 1) * LANES]

        sub_block(*SUB_BLOCKS[0])
        for start, rows in SUB_BLOCKS[1:]:
            pl.when(start < n)(functools.partial(sub_block, start, rows))


def _step(nv, t, j, last):
    return jnp.where(nv[t] > 0, j, last)


def _experts(layer, te, nv, tb, xs_rows, w1, b1, w2, b2):
    def w_spec(rows, cols, last, offset):
        return pl.BlockSpec((None, None, rows, cols),
                            lambda t, j, te, nv, tb: (layer, te[t], 0, offset + _step(nv, t, j, last)))

    b1r = b1.reshape(DEPTH, N_EXPERTS, 1, 2 * D_FF)
    h = pl.pallas_call(
        _expert1_kernel,
        grid_spec=pltpu.PrefetchScalarGridSpec(
            num_scalar_prefetch=3,
            grid=(N_ETILES, NJ1),
            in_specs=[
                pl.BlockSpec((ETM * ROW_TILES, LANES), lambda t, j, te, nv, tb: (tb[t], 0)),
                w_spec(D_MODEL, TF, NJ1 - 1, 0), w_spec(D_MODEL, TF, NJ1 - 1, NJ1),
                w_spec(1, TF, NJ1 - 1, 0), w_spec(1, TF, NJ1 - 1, NJ1),
            ],
            out_specs=pl.BlockSpec((ETM, TF), lambda t, j, te, nv, tb: (tb[t], _step(nv, t, j, NJ1 - 1))),
            scratch_shapes=[pltpu.VMEM((ETM, D_MODEL), BF16)],
        ),
        out_shape=jax.ShapeDtypeStruct((E_ROWS, D_FF), BF16),
        compiler_params=_cparams("arbitrary", "arbitrary"),
        name="moe_expert_in",
    )(te, nv, tb, xs_rows, w1, w1, b1r, b1r)
    return pl.pallas_call(
        _expert2_kernel,
        grid_spec=pltpu.PrefetchScalarGridSpec(
            num_scalar_prefetch=3,
            grid=(N_ETILES, NJ2),
            in_specs=[
                pl.BlockSpec((ETM, D_FF), lambda t, j, te, nv, tb: (tb[t], 0)),
                w_spec(D_FF, TN2, NJ2 - 1, 0), w_spec(1, TN2, NJ2 - 1, 0),
            ],
            out_specs=pl.BlockSpec((ETM * ROW_TILES, LANES), lambda t, j, te, nv, tb: (tb[t], 0)),
        ),
        out_shape=jax.ShapeDtypeStruct((E_ROWS * ROW_TILES, LANES), F32),
        compiler_params=_cparams("arbitrary", "arbitrary"),
        name="moe_expert_out",
    )(te, nv, tb, h, w2, b2.reshape(DEPTH, N_EXPERTS, 1, D_MODEL))


def _combine_kernel(final, dcur_ref, dnext_ref, y_hbm, x_ref, gate_ref, gf_ref, gfin_ref, o_ref, buf, sem):
    i = pl.program_id(0)
    n = pl.num_programs(0)
    slot = i % 2

    def issue(d_ref, s):
        def body(tl, carry):
            for k in range(TOP_K):
                src = d_ref[0, 0, tl * TOP_K + k]
                pltpu.make_async_copy(
                    y_hbm.at[pl.ds(pl.multiple_of(src * ROW_TILES, ROW_TILES), ROW_TILES)],
                    buf.at[s, pl.ds(pl.multiple_of((k * TM_COMB + tl) * ROW_PITCH, SUBLANES), ROW_TILES)],
                    sem.at[s]).start(priority=k % 2)
            return carry
        lax.fori_loop(0, TM_COMB, body, 0, unroll=2)

    @pl.when(i == 0)
    def _():
        issue(dcur_ref, 0)

    @pl.when(i + 1 < n)
    def _():
        issue(dnext_ref, 1 - slot)

    landed = buf.at[slot, pl.ds(0, N_COMB * ROW_TILES)]
    pltpu.make_async_copy(landed, landed, sem.at[slot]).wait()

    gates = gate_ref[...]
    x = x_ref[...]
    gf = gf_ref[0]
    pieces = []
    for lt in range(ROW_TILES):
        acc = jnp.zeros((TM_COMB, LANES), F32)
        for k in range(TOP_K):
            v = buf[slot, pl.ds(k * TM_COMB * ROW_PITCH + lt, TM_COMB, stride=ROW_PITCH), :]
            acc = acc + gates[:, k:k + 1] * v
        lc = slice(lt * LANES, (lt + 1) * LANES)
        pieces.append(x[:, lc] + gf[:, lc] * acc)
    out = jnp.concatenate(pieces, axis=-1)
    if final:
        out = _rms(out) * gfin_ref[...]
    o_ref[...] = out


def _combine(dest, y_rows, x_new, gates, mod, g_final, final):
    n_blk = N_TOK // TM_COMB
    p_tiles = N_P // TM_COMB
    d3 = dest.reshape(n_blk, 1, N_COMB)
    return pl.pallas_call(
        functools.partial(_combine_kernel, final),
        grid=(n_blk,),
        in_specs=[
            pl.BlockSpec((1, 1, N_COMB), lambda i: (i, 0, 0), memory_space=pltpu.SMEM),
            pl.BlockSpec((1, 1, N_COMB), lambda i: (jnp.minimum(i + 1, n_blk - 1), 0, 0), memory_space=pltpu.SMEM),
            pl.BlockSpec(memory_space=pl.ANY),
            pl.BlockSpec((TM_COMB, D_MODEL), lambda i: (i, 0)),
            pl.BlockSpec((TM_COMB, LANES), lambda i: (i, 0)),
            pl.BlockSpec((1, MOD_ROWS, D_MODEL), lambda i: (i // p_tiles, 0, 5)),
            _const_spec((1, D_MODEL)),
        ],
        out_specs=pl.BlockSpec((TM_COMB, D_MODEL), lambda i: (i, 0)),
        out_shape=jax.ShapeDtypeStruct((N_TOK, D_MODEL), F32),
        scratch_shapes=[pltpu.VMEM((2, N_COMB * ROW_PITCH, LANES), F32), pltpu.SemaphoreType.DMA((2,))],
        compiler_params=_cparams("arbitrary"),
        name="moe_combine",
    )(d3, d3, y_rows, x_new, gates, mod, g_final.reshape(1, D_MODEL))


def kernel(x_prompt, x_sample, c_prompt, c_sample, state_ssm_re, state_ssm_im, g_mix, w_ada, b_ada, w_in, lam_re, lam_im, log_dt, b_re, b_im, c_re, c_im, d_skip, w_glu, b_glu, ln_v_g, ln_v_b, w_s, b_s, g_out_a, g_out_b, w_out, g_ffn, w_router, b_router, w1, b1, w2, b2, g_final):
    x = jnp.concatenate([
        x_prompt.transpose(1, 0, 2).reshape(N_P, D_MODEL),
        x_sample.transpose(1, 0, 2).reshape(N_S, D_MODEL)], axis=0)

    c_all = jnp.concatenate([c_prompt, jnp.zeros((8 - BATCH, D_MODEL), F32), c_sample], axis=0)
    mod_all = _ada_mod(c_all, w_ada, b_ada)
    mod_tok = jnp.stack([jnp.tile(mod_all[:, :BATCH], (1, MOD_ROWS // BATCH, 1)), mod_all[:, 8:]], axis=1)

    ab_re, ab_im, bb_re, bb_im = _discretise(lam_re, lam_im, log_dt, b_re, b_im)
    bcat, ccat, abr, abi = _block_diag_params(ab_re, ab_im, bb_re, bb_im, c_re, c_im)

    r_idx = jnp.arange(TM_MIX, dtype=I32)
    spread = (r_idx[:, None] // BATCH == jnp.arange(CHUNK, dtype=I32)[None, :]).astype(F32)
    wk = jnp.einsum("rt,lhts,cs->lhrc", spread, w_s, spread, precision=lax.Precision.HIGHEST)
    wk = jnp.where(r_idx[:, None] % BATCH == r_idx[None, :] % BATCH, wk, 0.0).astype(BF16)
    bsr = jnp.repeat(b_s.transpose(0, 2, 1), BATCH, axis=1)
    ws4 = jnp.repeat(w_s[:, :, :DEC_SEQ, :DEC_SEQ].transpose(0, 2, 3, 1).reshape(
        DEPTH, DEC_SEQ * DEC_SEQ, N_GMLP_HEADS), GMLP_HEAD, axis=2)
    bs4 = jnp.repeat(b_s[:, :, :DEC_SEQ].transpose(0, 2, 1), GMLP_HEAD, axis=2)
    wr_pad = jnp.pad(w_router, ((0, 0), (0, 0), (0, LANES - N_EXPERTS)))
    br_pad = jnp.pad(b_router, ((0, 0), (0, LANES - N_EXPERTS)), constant_values=-1e30).reshape(DEPTH, 1, LANES)

    h0_re = state_ssm_re.reshape(DEPTH, DEC_BATCH, N_SSM_GROUPS * SSM_STATE)
    h0_im = state_ssm_im.reshape(DEPTH, DEC_BATCH, N_SSM_GROUPS * SSM_STATE)

    st_p_re, st_p_im, st_s_re, st_s_im, v_rows = [], [], [], [], []
    for l in range(DEPTH):
        mod = mod_tok[l]
        u_a, gu_b, vn = _mix_in(x, mod, g_mix[l], w_in[l].astype(BF16), ln_v_g[l], ln_v_b[l])
        wglu_bf = w_glu[l].astype(BF16)
        ya_p, hp_re, hp_im = _s5_prompt(u_a, bcat[l], ccat[l], abr[l], abi[l], d_skip[l], wglu_bf, b_glu[l])
        ya_s, hs_re, hs_im = _s5_sample(u_a, h0_re[l], h0_im[l], bcat[l], ccat[l], abr[l], abi[l], d_skip[l],
                                        wglu_bf, b_glu[l])
        y_a = jnp.concatenate([ya_p, ya_s], axis=0)
        y_b = _gmlp(gu_b, vn, wk[l], bsr[l], ws4[l], bs4[l])
        x_new, h2_rows, idx, gates = _mix_out(
            x, y_a, y_b, g_out_a[l], g_out_b[l], w_out[l].astype(BF16), mod, g_ffn[l], wr_pad[l], br_pad[l])
        dest, te, nv, tb, zrow = _route(idx[:, :TOP_K])
        xs_rows = _gather_rows(dest, zrow, h2_rows)
        y_rows = _experts(l, te, nv, tb, xs_rows, w1, b1, w2, b2)
        x = _combine(dest, y_rows, x_new, gates, mod, g_final, l == DEPTH - 1)

        unblock = lambda h: h[:, :BATCH].transpose(1, 0, 2).reshape(BATCH, N_SSM_GROUPS, SSM_STATE)
        st_p_re.append(unblock(hp_re))
        st_p_im.append(unblock(hp_im))
        st_s_re.append(hs_re.reshape(DEC_BATCH, N_SSM_GROUPS, SSM_STATE))
        st_s_im.append(hs_im.reshape(DEC_BATCH, N_SSM_GROUPS, SSM_STATE))
        v_rows.append(vn[N_P:].reshape(DEC_SEQ, DEC_BATCH, D_GMLP).transpose(1, 0, 2))

    y_prompt = x[:N_P].reshape(SEQ, BATCH, D_MODEL).transpose(1, 0, 2)
    y_sample = x[N_P:].reshape(DEC_SEQ, DEC_BATCH, D_MODEL).transpose(1, 0, 2)
    return (y_prompt, y_sample, jnp.stack(st_p_re), jnp.stack(st_p_im),
            jnp.stack(st_s_re), jnp.stack(st_s_im), jnp.stack(v_rows))
```
